```python
import math
import jax, jax.numpy as jnp
from jax import lax
import numpy as np


D_MODEL = 2048
BATCH = 8
SEQ = 2048
DEPTH = 1

CHUNK = 64
Q_BLOCK = 128
N_MEM = 256
HG_HEADS = 16
HG_KDIM = 128
HG_VDIM = D_MODEL // HG_HEADS
HG_K = HG_HEADS * HG_KDIM
HG_V = HG_HEADS * HG_VDIM
MLA_HEADS = 16
Q_LORA = 512
KV_LORA = 512
QK_NOPE = 128
QK_ROPE = 64
V_HEAD = 128
MLA_QK = QK_NOPE + QK_ROPE
MLA_V = MLA_HEADS * V_HEAD
ROPE_THETA = 10000.0
XA_HEADS = 4
XA_HEAD_DIM = 128
XA_WIDTH = XA_HEADS * XA_HEAD_DIM
D_FF = 5632
FFN_RESIDUAL_WEIGHT = 0.5
EPS = 1e-6
IN_SPLITS = (HG_K, HG_K, HG_V, HG_V, Q_LORA, KV_LORA, QK_ROPE, D_MODEL, D_MODEL)
IN_DIM = HG_K + HG_K + HG_V + HG_V + Q_LORA + KV_LORA + QK_ROPE + D_MODEL + D_MODEL

kernel_name = 'hybrid_hgrn2_mla_macaron_sandwich_memory_block'


def rms_norm(x, g):
    xf = x.astype(jnp.float32)
    y = xf * lax.rsqrt(jnp.mean(xf * xf, axis=-1, keepdims=True) + EPS)
    return (y * g.astype(jnp.float32)).astype(x.dtype)


def rotary(x, cos, sin):
    x1, x2 = jnp.split(x, 2, axis=-1)
    return jnp.concatenate([x1 * cos - x2 * sin, x2 * cos + x1 * sin], axis=-1)


def split_columns(u):
    parts, start = [], 0
    for width in IN_SPLITS:
        parts.append(u[..., start:start + width])
        start += width
    return parts


def swiglu_half_step(x, pre_g, w_gate, w_up, w_down, post_g):
    h = rms_norm(x, pre_g)
    y = (jax.nn.silu(h @ w_gate) * (h @ w_up)) @ w_down
    return x + FFN_RESIDUAL_WEIGHT * rms_norm(y, post_g)


def hgrn2_chunk_scan(q, k, log_f, v):
    B, S, H, K = q.shape
    V = v.shape[-1]
    n = S // CHUNK

    def to_chunks(t):
        return t.reshape(B, n, CHUNK, H, t.shape[-1]).transpose(1, 0, 3, 2, 4)

    causal = jnp.tril(jnp.ones((CHUNK, CHUNK), dtype=bool))

    def step(state, inp):
        qc, kc, gc, vc = inp
        b = jnp.cumsum(gc, axis=2)
        diff = b[:, :, :, None, :] - b[:, :, None, :, :]
        decay = jnp.exp(jnp.where(causal[None, None, :, :, None], diff, -jnp.inf))
        scores = jnp.einsum('bhtk,bhsk,bhtsk->bhts', qc, kc, decay)
        o = (jnp.einsum('bhts,bhsv->bhtv', scores, vc)
             + jnp.einsum('bhtk,bhkv->bhtv', qc * jnp.exp(b), state))
        b_last = b[:, :, -1:, :]
        new_state = (jnp.exp(b_last[:, :, 0, :])[..., None] * state
                     + jnp.einsum('bhsk,bhsv->bhkv', kc * jnp.exp(b_last - b), vc))
        return new_state, o

    state0 = jnp.zeros((B, H, K, V), jnp.float32)
    _, o = lax.scan(step, state0, (to_chunks(q), to_chunks(k), to_chunks(log_f), to_chunks(v)))
    return o.transpose(1, 0, 3, 2, 4).reshape(B, S, H, V)


def chunk_causal_attention(q, k, v, scale):
    B, S, H, Dqk = q.shape
    Dv = v.shape[-1]
    nblk = S // Q_BLOCK
    qb = q.reshape(B, nblk, Q_BLOCK, H, Dqk).transpose(1, 0, 2, 3, 4)
    key_chunk = jnp.arange(S) // CHUNK

    def one_block(args):
        blk, q_blk = args
        query_chunk = (blk * Q_BLOCK + jnp.arange(Q_BLOCK)) // CHUNK
        mask = key_chunk[None, :] <= query_chunk[:, None]
        s = jnp.einsum('bqhd,bkhd->bhqk', q_blk, k, preferred_element_type=jnp.float32) * scale
        s = jnp.where(mask[None, None], s, -jnp.inf)
        p = jax.nn.softmax(s, axis=-1).astype(v.dtype)
        return jnp.einsum('bhqk,bkhd->bqhd', p, v)

    out = lax.map(one_block, (jnp.arange(nblk), qb))
    return out.transpose(1, 0, 2, 3, 4).reshape(B, S, H, Dv)


def hybrid_mixer(x, cos, sin, lb, pre_g, w_in, hg_norm_g, q_norm_g, w_q_up, kv_norm_g,
                 w_kv_up, w_branch_a, w_branch_b, w_out, post_g):
    B, S, _ = x.shape
    f32 = jnp.float32
    h = rms_norm(x, pre_g)
    u = h @ w_in
    q_hg, f_hg, i_hg, og_hg, c_q, c_kv, k_pe, gate_a, gate_b = split_columns(u)

    f_raw = f_hg.astype(f32).reshape(B, S, HG_HEADS, HG_KDIM)
    lb_h = lb.reshape(HG_HEADS, HG_KDIM)
    log_f = jnp.logaddexp(jnp.log(lb_h), jnp.log1p(-lb_h) + jax.nn.log_sigmoid(f_raw))
    k_in = (1.0 - lb_h) * jax.nn.sigmoid(-f_raw)
    q_in = jax.nn.silu(q_hg.astype(f32)).reshape(B, S, HG_HEADS, HG_KDIM)
    v_in = i_hg.astype(f32).reshape(B, S, HG_HEADS, HG_VDIM)
    o_a = hgrn2_chunk_scan(q_in, k_in, log_f, v_in).astype(x.dtype)
    o_a = (rms_norm(o_a, hg_norm_g.reshape(HG_HEADS, HG_VDIM))
           * jax.nn.silu(og_hg).reshape(B, S, HG_HEADS, HG_VDIM))
    y_a = o_a.reshape(B, S, HG_V) @ w_branch_a

    q = (rms_norm(c_q, q_norm_g) @ w_q_up).reshape(B, S, MLA_HEADS, MLA_QK)
    q_nope, q_pe = q[..., :QK_NOPE], q[..., QK_NOPE:]
    q_pe = rotary(q_pe, cos[:, :, None, :], sin[:, :, None, :])
    kv = (rms_norm(c_kv, kv_norm_g) @ w_kv_up).reshape(B, S, MLA_HEADS, QK_NOPE + V_HEAD)
    k_nope, v = kv[..., :QK_NOPE], kv[..., QK_NOPE:]
    k_pe = rotary(k_pe, cos, sin)
    q_full = jnp.concatenate([q_nope, q_pe], axis=-1)
    k_full = jnp.concatenate(
        [k_nope, jnp.broadcast_to(k_pe[:, :, None, :], (B, S, MLA_HEADS, QK_ROPE))], axis=-1)
    o_b = chunk_causal_attention(q_full, k_full, v, MLA_QK ** -0.5)
    y_b = o_b.reshape(B, S, MLA_V) @ w_branch_b

    y = jax.nn.sigmoid(gate_a) * y_a + jax.nn.sigmoid(gate_b) * y_b
    return x + rms_norm(y @ w_out, post_g)


def memory_cross_attention(x, mem, pre_g, mem_g, w_q, w_k, w_v, w_o, post_g):
    B, S, _ = x.shape
    M = mem.shape[1]
    h = rms_norm(x, pre_g)
    m = rms_norm(mem, mem_g)
    q = (h @ w_q).reshape(B, S, XA_HEADS, XA_HEAD_DIM)
    k = (m @ w_k).reshape(B, M, XA_HEADS, XA_HEAD_DIM)
    v = (m @ w_v).reshape(B, M, XA_HEADS, XA_HEAD_DIM)
    s = jnp.einsum('bqhd,bkhd->bhqk', q, k, preferred_element_type=jnp.float32) * XA_HEAD_DIM ** -0.5
    p = jax.nn.softmax(s, axis=-1).astype(v.dtype)
    o = jnp.einsum('bhqk,bkhd->bqhd', p, v).reshape(B, S, XA_WIDTH)
    return x + rms_norm(o @ w_o, post_g)


def setup_inputs(seed: int = 0) -> dict:
    key = jax.random.key(seed)
    ks = iter(jax.random.split(key, 40))
    L = DEPTH

    def w(shape, fan_in):
        return jax.random.normal(next(ks), shape, jnp.float32) * fan_in ** -0.5

    def gain(n):
        return 1.0 + 0.02 * jax.random.normal(next(ks), (L, n), jnp.float32)

    x = jax.random.normal(next(ks), (BATCH, SEQ, D_MODEL), jnp.float32)
    mem = jax.random.normal(next(ks), (BATCH, N_MEM, D_MODEL), jnp.float32)
    offset = jax.random.randint(next(ks), (BATCH, 1), 0, 64, dtype=jnp.int32) * CHUNK
    positions = (offset + jnp.arange(SEQ, dtype=jnp.int32)[None, :]).astype(jnp.int32)
    hgrn_lb_logits = 0.5 * jax.random.normal(next(ks), (L + 1, HG_K), jnp.float32)
    return {
        'x': x, 'mem': mem, 'positions': positions, 'hgrn_lb_logits': hgrn_lb_logits,
        'ffn1_pre_g': gain(D_MODEL),
        'ffn1_w_gate': w((L, D_MODEL, D_FF), D_MODEL),
        'ffn1_w_up': w((L, D_MODEL, D_FF), D_MODEL),
        'ffn1_w_down': w((L, D_FF, D_MODEL), D_FF),
        'ffn1_post_g': gain(D_MODEL),
        'mix_pre_g': gain(D_MODEL),
        'w_in': w((L, D_MODEL, IN_DIM), D_MODEL),
        'hg_norm_g': gain(HG_V),
        'mla_q_norm_g': gain(Q_LORA),
        'mla_w_q_up': w((L, Q_LORA, MLA_HEADS * MLA_QK), Q_LORA),
        'mla_kv_norm_g': gain(KV_LORA),
        'mla_w_kv_up': w((L, KV_LORA, MLA_HEADS * (QK_NOPE + V_HEAD)), KV_LORA),
        'w_branch_a': w((L, HG_V, D_MODEL), HG_V),
        'w_branch_b': w((L, MLA_V, D_MODEL), MLA_V),
        'w_out': w((L, D_MODEL, D_MODEL), D_MODEL),
        'mix_post_g': gain(D_MODEL),
        'xa_pre_g': gain(D_MODEL),
        'xa_mem_g': gain(D_MODEL),
        'xa_w_q': w((L, D_MODEL, XA_WIDTH), D_MODEL),
        'xa_w_k': w((L, D_MODEL, XA_WIDTH), D_MODEL),
        'xa_w_v': w((L, D_MODEL, XA_WIDTH), D_MODEL),
        'xa_w_o': w((L, XA_WIDTH, D_MODEL), XA_WIDTH),
        'xa_post_g': gain(D_MODEL),
        'ffn2_pre_g': gain(D_MODEL),
        'ffn2_w_gate': w((L, D_MODEL, D_FF), D_MODEL),
        'ffn2_w_up': w((L, D_MODEL, D_FF), D_MODEL),
        'ffn2_w_down': w((L, D_FF, D_MODEL), D_FF),
        'ffn2_post_g': gain(D_MODEL),
    }


def reference(x, mem, positions, hgrn_lb_logits,
              ffn1_pre_g, ffn1_w_gate, ffn1_w_up, ffn1_w_down, ffn1_post_g,
              mix_pre_g, w_in, hg_norm_g, mla_q_norm_g, mla_w_q_up, mla_kv_norm_g, mla_w_kv_up,
              w_branch_a, w_branch_b, w_out, mix_post_g,
              xa_pre_g, xa_mem_g, xa_w_q, xa_w_k, xa_w_v, xa_w_o, xa_post_g,
              ffn2_pre_g, ffn2_w_gate, ffn2_w_up, ffn2_w_down, ffn2_post_g):
    f32 = jnp.float32
    inv_freq = 1.0 / (ROPE_THETA ** (jnp.arange(0, QK_ROPE, 2, dtype=f32) / QK_ROPE))
    ang = positions.astype(f32)[..., None] * inv_freq
    cos = jnp.cos(ang).astype(x.dtype)
    sin = jnp.sin(ang).astype(x.dtype)
    lower_bounds = jnp.cumsum(jax.nn.softmax(hgrn_lb_logits.astype(f32), axis=0), axis=0)

    for l in range(DEPTH):
        x = swiglu_half_step(x, ffn1_pre_g[l], ffn1_w_gate[l], ffn1_w_up[l], ffn1_w_down[l], ffn1_post_g[l])
        x = hybrid_mixer(x, cos, sin, lower_bounds[l], mix_pre_g[l], w_in[l], hg_norm_g[l],
                         mla_q_norm_g[l], mla_w_q_up[l], mla_kv_norm_g[l], mla_w_kv_up[l],
                         w_branch_a[l], w_branch_b[l], w_out[l], mix_post_g[l])
        x = memory_cross_attention(x, mem, xa_pre_g[l], xa_mem_g[l], xa_w_q[l], xa_w_k[l],
                                   xa_w_v[l], xa_w_o[l], xa_post_g[l])
        x = swiglu_half_step(x, ffn2_pre_g[l], ffn2_w_gate[l], ffn2_w_up[l], ffn2_w_down[l], ffn2_post_g[l])
    return x
```

```python
import functools

import jax
import jax.numpy as jnp
from jax import lax
from jax.experimental import pallas as pl
from jax.experimental.pallas import tpu as pltpu

F32 = jnp.float32
BF16 = jnp.bfloat16

CHUNK = 64
HG_HEADS = 16
MLA_HEADS = 16
QK_NOPE = 128
QK_ROPE = 64
V_HEAD = 128
XA_HEADS = 4
ROPE_THETA = 10000.0
FFN_RESIDUAL_WEIGHT = 0.5
EPS = 1e-6

LANES = 128
MIB = 1024 * 1024
VMEM_LIMIT_BYTES = 56 * MIB

HGRN_CHUNK = 128
ATTN_Q_BLOCK = 256


def _cparams(*sem):
    return pltpu.CompilerParams(dimension_semantics=sem, vmem_limit_bytes=VMEM_LIMIT_BYTES)


def _rms(x, g):
    return x * lax.rsqrt(jnp.mean(x * x, axis=-1, keepdims=True) + EPS) * g


def _sigmoid(x):
    return 1.0 / (1.0 + jnp.exp(-x))


def _dot(a, b):
    return jnp.dot(a, b, preferred_element_type=F32)


def _dot_nt(a, b):
    return lax.dot_general(a, b, (((1,), (1,)), ((), ())), preferred_element_type=F32)


def _dot_tn(a, b):
    return lax.dot_general(a, b, (((0,), (0,)), ((), ())), preferred_element_type=F32)


def _resident(shape):
    return pl.BlockSpec(shape, lambda *_: (0,) * len(shape), pipeline_mode=pl.Buffered(1))


def _ffn_kernel(x_ref, pg_ref, wg_ref, wu_ref, wd_ref, og_ref, o_ref, h_ref, acc_ref):
    j = pl.program_id(1)

    @pl.when(j == 0)
    def _():
        h_ref[...] = _rms(x_ref[...], pg_ref[...]).astype(BF16)
        acc_ref[...] = jnp.zeros_like(acc_ref)

    h = h_ref[...]
    g = _dot(h, wg_ref[...])
    u = _dot(h, wu_ref[...])
    a = (g * _sigmoid(g) * u).astype(BF16)
    acc_ref[...] += _dot(a, wd_ref[...])

    @pl.when(j == pl.num_programs(1) - 1)
    def _():
        o_ref[...] = x_ref[...] + FFN_RESIDUAL_WEIGHT * _rms(acc_ref[...], og_ref[...])


def _ffn(x, pre_g, w_gate, w_up, w_down, post_g, *, tm=512, tf=512):
    t, d = x.shape
    f = w_gate.shape[1]
    return pl.pallas_call(
        _ffn_kernel,
        grid=(t // tm, f // tf),
        in_specs=[
            pl.BlockSpec((tm, d), lambda i, j: (i, 0)),
            pl.BlockSpec((1, d), lambda i, j: (0, 0)),
            pl.BlockSpec((d, tf), lambda i, j: (0, j)),
            pl.BlockSpec((d, tf), lambda i, j: (0, j)),
            pl.BlockSpec((tf, d), lambda i, j: (j, 0)),
            pl.BlockSpec((1, d), lambda i, j: (0, 0)),
        ],
        out_specs=pl.BlockSpec((tm, d), lambda i, j: (i, 0)),
        out_shape=jax.ShapeDtypeStruct((t, d), F32),
        scratch_shapes=[pltpu.VMEM((tm, d), BF16), pltpu.VMEM((tm, d), F32)],
        compiler_params=_cparams("parallel", "arbitrary"),
        name="ffn",
    )(x, pre_g, w_gate, w_up, w_down, post_g)


def _norm_mm_kernel(x_ref, g_ref, w_ref, o_ref, h_ref):
    @pl.when(pl.program_id(1) == 0)
    def _():
        h_ref[...] = _rms(x_ref[...].astype(F32), g_ref[...]).astype(BF16)

    o_ref[...] = _dot(h_ref[...], w_ref[...]).astype(o_ref.dtype)


def _norm_mm(x, g, w, out_dtype, *, tm, tn, name):
    t, k = x.shape
    n = w.shape[1]
    return pl.pallas_call(
        _norm_mm_kernel,
        grid=(t // tm, n // tn),
        in_specs=[
            pl.BlockSpec((tm, k), lambda i, j: (i, 0)),
            pl.BlockSpec((1, k), lambda i, j: (0, 0)),
            pl.BlockSpec((k, tn), lambda i, j: (0, j)),
        ],
        out_specs=pl.BlockSpec((tm, tn), lambda i, j: (i, j)),
        out_shape=jax.ShapeDtypeStruct((t, n), out_dtype),
        scratch_shapes=[pltpu.VMEM((tm, k), BF16)],
        compiler_params=_cparams("parallel", "arbitrary"),
        name=name,
    )(x, g, w)


def _bcast_block_row(x, rows, m):
    c, lanes = x.shape
    if m >= 4:
        x3 = x.reshape(c // (2 * m), 2 * m, lanes)
        r3 = jnp.broadcast_to(x3[:, m - 1:m, :], x3.shape)
        return r3.reshape(c, lanes)
    if m == 2:
        r = rows & 3
        return jnp.where(r == 0, pltpu.roll(x, c - 1, 0),
                         jnp.where(r == 1, x,
                                   jnp.where(r == 2, pltpu.roll(x, 1, 0), pltpu.roll(x, 2, 0))))
    return jnp.where((rows & 1) == 1, pltpu.roll(x, 1, 0), x)


def _hgrn_kernel(q_ref, f_ref, i_ref, og_ref, lb_ref, ng_ref, o_ref, *, chunk):
    seq, kdim = q_ref.shape
    n_chunks = seq // chunk
    lb = lb_ref[...]
    log_lb = jnp.log(lb)
    log1m_lb = jnp.log1p(-lb)
    one_m_lb = 1.0 - lb
    ng = ng_ref[...]
    rows = lax.broadcasted_iota(jnp.int32, (chunk, kdim), 0)
    tt = lax.broadcasted_iota(jnp.int32, (chunk, chunk), 0)
    ss = lax.broadcasted_iota(jnp.int32, (chunk, chunk), 1)
    pair_bits = jnp.where(tt > ss, tt ^ ss, 0)

    def body(c, st):
        sl = pl.ds(pl.multiple_of(c * chunk, chunk), chunk)
        fr = f_ref[sl, :]
        softplus_tail = jnp.log1p(jnp.exp(-jnp.abs(fr)))
        log_sig = jnp.minimum(fr, 0.0) - softplus_tail
        log_sig_neg = jnp.minimum(-fr, 0.0) - softplus_tail
        lhs = log1m_lb + log_sig
        g = jnp.maximum(log_lb, lhs) + jnp.log1p(jnp.exp(-jnp.abs(log_lb - lhs)))
        kk = one_m_lb * jnp.exp(log_sig_neg)
        qf = q_ref[sl, :].astype(F32)
        qq = qf * _sigmoid(qf)
        vv = i_ref[sl, :]

        x = g
        a = jnp.zeros((chunk, chunk), F32)
        m = 1
        while m < chunk:
            ref_row = _bcast_block_row(x, rows, m)
            upper = (rows & m) != 0
            w = jnp.exp(jnp.where(upper, x, ref_row - x))
            p = _dot_nt((qq * w).astype(BF16), (kk * w).astype(BF16))
            a = jnp.where((pair_bits >= m) & (pair_bits < 2 * m), p, a)
            x = jnp.where(upper, x + ref_row, x)
            m *= 2
        b = x
        a = jnp.where(tt == ss, jnp.sum(qq * kk, axis=-1, keepdims=True), a)

        o = _dot(a.astype(BF16), vv) + _dot_nt((qq * jnp.exp(b)).astype(BF16), st.astype(BF16))
        b_last = b[chunk - 1:chunk, :]
        kd = (kk * jnp.exp(b_last - b)).astype(BF16)
        st = jnp.exp(b_last) * st + _dot_tn(vv, kd)

        og = og_ref[sl, :].astype(F32)
        o_ref[sl, :] = (_rms(o, ng) * (og * _sigmoid(og))).astype(o_ref.dtype)
        return st

    lax.fori_loop(0, n_chunks, body, jnp.zeros((kdim, kdim), F32))


def _hgrn(u, f_raw, lb, norm_g, *, batch, seq):
    t, width = f_raw.shape
    hd = width // HG_HEADS
    u3 = u.reshape(batch, seq, u.shape[1])
    f3 = f_raw.reshape(batch, seq, width)

    def col(group):
        return pl.BlockSpec((None, seq, hd), lambda b, h, g=group: (b, 0, g * HG_HEADS + h))

    out = pl.pallas_call(
        functools.partial(_hgrn_kernel, chunk=HGRN_CHUNK),
        grid=(batch, HG_HEADS),
        in_specs=[
            col(0),
            pl.BlockSpec((None, seq, hd), lambda b, h: (b, 0, h)),
            col(1),
            col(2),
            pl.BlockSpec((1, hd), lambda b, h: (0, h)),
            pl.BlockSpec((1, hd), lambda b, h: (0, h)),
        ],
        out_specs=pl.BlockSpec((None, seq, hd), lambda b, h: (b, 0, h)),
        out_shape=jax.ShapeDtypeStruct((batch, seq, width), BF16),
        compiler_params=_cparams("parallel", "parallel"),
        name="hgrn2",
    )(u3, f3, u3, u3, lb, norm_g)
    return out.reshape(t, width)


def _rotate(pair, cs):
    y = pair * cs
    return y + pltpu.roll(y, QK_ROPE, 1)


def _q_up_kernel(c_ref, g_ref, w_ref, cs_ref, o_ref):
    h = _rms(c_ref[...].astype(F32), g_ref[...]).astype(BF16)
    cs = cs_ref[...]
    head_w = QK_NOPE + 2 * QK_ROPE
    for hd in range(MLA_HEADS):
        lo = hd * head_w
        acc = _dot(h, w_ref[:, lo:lo + head_w])
        o_ref[:, lo:lo + QK_NOPE] = acc[:, :QK_NOPE].astype(o_ref.dtype)
        o_ref[:, lo + QK_NOPE:lo + head_w] = _rotate(acc[:, QK_NOPE:], cs).astype(o_ref.dtype)


def _q_up(cgroup, g, w, cs, *, tm=512):
    t = cgroup.shape[0]
    k = g.shape[1]
    n = w.shape[1]
    return pl.pallas_call(
        _q_up_kernel,
        grid=(t // tm,),
        in_specs=[
            pl.BlockSpec((tm, k), lambda i: (i, 0)),
            _resident((1, k)),
            _resident((k, n)),
            pl.BlockSpec((tm, LANES), lambda i: (i, 0)),
        ],
        out_specs=pl.BlockSpec((tm, n), lambda i: (i, 0)),
        out_shape=jax.ShapeDtypeStruct((t, n), BF16),
        compiler_params=_cparams("parallel"),
        name="mla_q_up",
    )(cgroup, g, w, cs)


def _kv_up_kernel(ckv_ref, kpe_ref, g_ref, w_ref, cs_ref, k_ref, v_ref):
    h = _rms(ckv_ref[...].astype(F32), g_ref[...]).astype(BF16)
    rot = _rotate(kpe_ref[...].astype(F32), cs_ref[...])
    lane = lax.broadcasted_iota(jnp.int32, rot.shape, 1)
    k_pe = jnp.where(lane < QK_ROPE, rot, 0.0).astype(k_ref.dtype)
    head_w = QK_NOPE + 2 * QK_ROPE
    nope_w = MLA_HEADS * QK_NOPE
    k_nope = _dot(h, w_ref[:, :nope_w])
    for hd in range(MLA_HEADS):
        lo = hd * head_w
        k_ref[:, lo:lo + QK_NOPE] = k_nope[:, hd * QK_NOPE:(hd + 1) * QK_NOPE].astype(k_ref.dtype)
        k_ref[:, lo + QK_NOPE:lo + head_w] = k_pe
    v_ref[...] = _dot(h, w_ref[:, nope_w:]).astype(v_ref.dtype)


def _kv_up(cgroup, g, w, cs, *, tm=512):
    t = cgroup.shape[0]
    k = g.shape[1]
    n_k = MLA_HEADS * (QK_NOPE + 2 * QK_ROPE)
    n_v = MLA_HEADS * V_HEAD
    return pl.pallas_call(
        _kv_up_kernel,
        grid=(t // tm,),
        in_specs=[
            pl.BlockSpec((tm, k), lambda i: (i, 1)),
            pl.BlockSpec((tm, LANES), lambda i: (i, 2 * k // LANES)),
            _resident((1, k)),
            _resident((k, w.shape[1])),
            pl.BlockSpec((tm, LANES), lambda i: (i, 0)),
        ],
        out_specs=[pl.BlockSpec((tm, n_k), lambda i: (i, 0)),
                   pl.BlockSpec((tm, n_v), lambda i: (i, 0))],
        out_shape=[jax.ShapeDtypeStruct((t, n_k), BF16), jax.ShapeDtypeStruct((t, n_v), BF16)],
        compiler_params=_cparams("parallel"),
        name="mla_kv_up",
    )(cgroup, cgroup, g, w, cs)


def _mla_attn_kernel(q_ref, k_ref, v_ref, o_ref, *, scale, tq, chunk):
    seq = q_ref.shape[0]
    row_chunk = lax.broadcasted_iota(jnp.int32, (tq, tq), 0) // chunk
    col_chunk = lax.broadcasted_iota(jnp.int32, (tq, tq), 1) // chunk
    visible = col_chunk <= row_chunk
    for iq in range(seq // tq):
        lo = iq * tq
        q = q_ref[lo:lo + tq, :]
        s_diag = jnp.where(visible, _dot_nt(q, k_ref[lo:lo + tq, :]) * scale, -jnp.inf)
        m = jnp.max(s_diag, axis=-1, keepdims=True)
        if iq > 0:
            s_past = _dot_nt(q, k_ref[0:lo, :]) * scale
            m = jnp.maximum(m, jnp.max(s_past, axis=-1, keepdims=True))
            p_past = jnp.exp(s_past - m)
            denom = jnp.sum(p_past, axis=-1, keepdims=True)
            acc = _dot(p_past.astype(BF16), v_ref[0:lo, :])
        p_diag = jnp.exp(s_diag - m)
        if iq > 0:
            denom = denom + jnp.sum(p_diag, axis=-1, keepdims=True)
            acc = acc + _dot(p_diag.astype(BF16), v_ref[lo:lo + tq, :])
        else:
            denom = jnp.sum(p_diag, axis=-1, keepdims=True)
            acc = _dot(p_diag.astype(BF16), v_ref[lo:lo + tq, :])
        o_ref[lo:lo + tq, :] = (acc * (1.0 / denom)).astype(o_ref.dtype)


def _mla_attn(q, k, v, *, batch, seq):
    t = q.shape[0]
    head_w = QK_NOPE + 2 * QK_ROPE
    q3 = q.reshape(batch, seq, q.shape[1])
    k3 = k.reshape(batch, seq, k.shape[1])
    v3 = v.reshape(batch, seq, v.shape[1])
    out = pl.pallas_call(
        functools.partial(_mla_attn_kernel, scale=(QK_NOPE + QK_ROPE) ** -0.5,
                          tq=ATTN_Q_BLOCK, chunk=CHUNK),
        grid=(batch, MLA_HEADS),
        in_specs=[
            pl.BlockSpec((None, seq, head_w), lambda b, h: (b, 0, h)),
            pl.BlockSpec((None, seq, head_w), lambda b, h: (b, 0, h)),
            pl.BlockSpec((None, seq, V_HEAD), lambda b, h: (b, 0, h)),
        ],
        out_specs=pl.BlockSpec((None, seq, V_HEAD), lambda b, h: (b, 0, h)),
        out_shape=jax.ShapeDtypeStruct((batch, seq, MLA_HEADS * V_HEAD), BF16),
        compiler_params=_cparams("parallel", "parallel"),
        name="mla_attn",
    )(q3, k3, v3)
    return out.reshape(t, MLA_HEADS * V_HEAD)


def _merge_kernel(oa_ref, ob_ref, ga_ref, gb_ref, x_ref, wa_ref, wb_ref, wo_ref, pg_ref, o_ref):
    ya = _dot(oa_ref[...], wa_ref[...])
    yb = _dot(ob_ref[...], wb_ref[...])
    y = _sigmoid(ga_ref[...].astype(F32)) * ya + _sigmoid(gb_ref[...].astype(F32)) * yb
    z = _dot(y.astype(BF16), wo_ref[...])
    o_ref[...] = x_ref[...] + _rms(z, pg_ref[...])


def _merge(o_a, o_b, u, x, w_a, w_b, w_o, post_g, *, gate_col, tm=256):
    t, d = x.shape
    return pl.pallas_call(
        _merge_kernel,
        grid=(t // tm,),
        in_specs=[
            pl.BlockSpec((tm, d), lambda i: (i, 0)),
            pl.BlockSpec((tm, d), lambda i: (i, 0)),
            pl.BlockSpec((tm, d), lambda i: (i, gate_col)),
            pl.BlockSpec((tm, d), lambda i: (i, gate_col + 1)),
            pl.BlockSpec((tm, d), lambda i: (i, 0)),
            _resident(w_a.shape),
            _resident(w_b.shape),
            _resident(w_o.shape),
            _resident((1, d)),
        ],
        out_specs=pl.BlockSpec((tm, d), lambda i: (i, 0)),
        out_shape=jax.ShapeDtypeStruct((t, d), F32),
        compiler_params=_cparams("parallel"),
        name="merge",
    )(o_a, o_b, u, u, x, w_a, w_b, w_o, post_g)


def _xattn_kernel(x_ref, pg_ref, wq_ref, kv_ref, wo_ref, og_ref, o_ref, *, scale):
    x = x_ref[...]
    q = _dot(_rms(x, pg_ref[...]).astype(BF16), wq_ref[...]).astype(BF16)
    width = wq_ref.shape[1]
    hd = width // XA_HEADS
    heads = []
    for h in range(XA_HEADS):
        s = _dot_nt(q[:, h * hd:(h + 1) * hd], kv_ref[:, h * hd:(h + 1) * hd]) * scale
        p = jnp.exp(s - jnp.max(s, axis=-1, keepdims=True))
        denom = jnp.sum(p, axis=-1, keepdims=True)
        o = _dot(p.astype(BF16), kv_ref[:, width + h * hd:width + (h + 1) * hd])
        heads.append((o * (1.0 / denom)).astype(BF16))
    z = _dot(jnp.concatenate(heads, axis=-1), wo_ref[...])
    o_ref[...] = x + _rms(z, og_ref[...])


def _xattn(x, pre_g, w_q, kv_mem, w_o, post_g, *, batch, seq, n_mem, tm=512):
    t, d = x.shape
    width = w_q.shape[1]
    x3 = x.reshape(batch, seq, d)
    kv3 = kv_mem.reshape(batch, n_mem, 2 * width)
    out = pl.pallas_call(
        functools.partial(_xattn_kernel, scale=(width // XA_HEADS) ** -0.5),
        grid=(batch, seq // tm),
        in_specs=[
            pl.BlockSpec((None, tm, d), lambda b, i: (b, i, 0)),
            _resident((1, d)),
            _resident(w_q.shape),
            pl.BlockSpec((None, n_mem, 2 * width), lambda b, i: (b, 0, 0)),
            _resident(w_o.shape),
            _resident((1, d)),
        ],
        out_specs=pl.BlockSpec((None, tm, d), lambda b, i: (b, i, 0)),
        out_shape=jax.ShapeDtypeStruct((batch, seq, d), F32),
        compiler_params=_cparams("parallel", "parallel"),
        name="xattn",
    )(x3, pre_g, w_q, kv3, w_o, post_g)
    return out.reshape(t, d)


def _rotate_half_columns(w):
    half = w.shape[-1] // 2
    return jnp.concatenate([-w[..., half:], w[..., :half]], axis=-1)


def kernel(x, mem, positions, hgrn_lb_logits, ffn1_pre_g, ffn1_w_gate, ffn1_w_up, ffn1_w_down, ffn1_post_g, mix_pre_g, w_in, hg_norm_g, mla_q_norm_g, mla_w_q_up, mla_kv_norm_g, mla_w_kv_up, w_branch_a, w_branch_b, w_out, mix_post_g, xa_pre_g, xa_mem_g, xa_w_q, xa_w_k, xa_w_v, xa_w_o, xa_post_g, ffn2_pre_g, ffn2_w_gate, ffn2_w_up, ffn2_w_down, ffn2_post_g):
    batch, seq, d = x.shape
    n_mem = mem.shape[1]
    depth = ffn1_w_gate.shape[0]
    t = batch * seq
    q_lora = mla_q_norm_g.shape[1]
    kv_lora = mla_kv_norm_g.shape[1]
    hg_k = hgrn_lb_logits.shape[1]
    hg_v = hg_norm_g.shape[1]

    inv_freq = 1.0 / (ROPE_THETA ** (jnp.arange(0, QK_ROPE, 2, dtype=F32) / QK_ROPE))
    ang = positions.astype(F32)[..., None] * inv_freq
    cos, sin = jnp.cos(ang), jnp.sin(ang)
    cs = jnp.concatenate([cos, cos, sin, sin], axis=-1).reshape(t, 2 * QK_ROPE)
    lower_bounds = jnp.cumsum(jax.nn.softmax(hgrn_lb_logits.astype(F32), axis=0), axis=0)

    o_q, o_f, o_i, o_og = 0, hg_k, 2 * hg_k, 2 * hg_k + hg_v
    o_cq = o_og + hg_v
    o_ckv = o_cq + q_lora
    o_kpe = o_ckv + kv_lora
    o_ga = o_kpe + QK_ROPE
    o_gb = o_ga + d

    h = x.reshape(t, d)
    mem2 = mem.reshape(batch * n_mem, d)
    for l in range(depth):
        row = lambda g: g[l][None, :]
        bf = lambda w: w.astype(BF16)

        h = _ffn(h, row(ffn1_pre_g), bf(ffn1_w_gate[l]), bf(ffn1_w_up[l]), bf(ffn1_w_down[l]),
                 row(ffn1_post_g))

        wi = w_in[l]
        w_u = bf(jnp.concatenate([wi[:, o_q:o_q + hg_k], wi[:, o_i:o_i + hg_v], wi[:, o_og:o_og + hg_v],
                                  wi[:, o_ga:o_ga + d], wi[:, o_gb:o_gb + d]], axis=1))
        w_f = bf(wi[:, o_f:o_f + hg_k])
        w_kpe = wi[:, o_kpe:o_kpe + QK_ROPE]
        w_c = bf(jnp.concatenate([wi[:, o_cq:o_cq + q_lora], wi[:, o_ckv:o_ckv + kv_lora],
                                  w_kpe, _rotate_half_columns(w_kpe)], axis=1))
        pre = row(mix_pre_g)
        u = _norm_mm(h, pre, w_u, BF16, tm=512, tn=1024, name="in_proj_u")
        f_raw = _norm_mm(h, pre, w_f, F32, tm=512, tn=1024, name="in_proj_f")
        cgroup = _norm_mm(h, pre, w_c, BF16, tm=512, tn=w_c.shape[1], name="in_proj_c")

        o_a = _hgrn(u, f_raw, lower_bounds[l][None, :], row(hg_norm_g), batch=batch, seq=seq)

        wq = mla_w_q_up[l].reshape(q_lora, MLA_HEADS, QK_NOPE + QK_ROPE)
        wq_pe = wq[..., QK_NOPE:]
        wq_full = bf(jnp.concatenate([wq[..., :QK_NOPE], wq_pe, _rotate_half_columns(wq_pe)], axis=-1)
                     .reshape(q_lora, -1))
        wkv = mla_w_kv_up[l].reshape(kv_lora, MLA_HEADS, QK_NOPE + V_HEAD)
        wkv_full = bf(jnp.concatenate([wkv[..., :QK_NOPE].reshape(kv_lora, -1),
                                       wkv[..., QK_NOPE:].reshape(kv_lora, -1)], axis=1))
        q_full = _q_up(cgroup, row(mla_q_norm_g), wq_full, cs)
        k_full, v = _kv_up(cgroup, row(mla_kv_norm_g), wkv_full, cs)
        o_b = _mla_attn(q_full, k_full, v, batch=batch, seq=seq)

        h = _merge(o_a, o_b, u, h, bf(w_branch_a[l]), bf(w_branch_b[l]), bf(w_out[l]),
                   row(mix_post_g), gate_col=(hg_k + 2 * hg_v) // d)

        w_kv_mem = bf(jnp.concatenate([xa_w_k[l], xa_w_v[l]], axis=1))
        kv_mem = _norm_mm(mem2, row(xa_mem_g), w_kv_mem, BF16, tm=512, tn=w_kv_mem.shape[1],
                          name="mem_kv")
        h = _xattn(h, row(xa_pre_g), bf(xa_w_q[l]), kv_mem, bf(xa_w_o[l]), row(xa_post_g),
                   batch=batch, seq=seq, n_mem=n_mem)

        h = _ffn(h, row(ffn2_pre_g), bf(ffn2_w_gate[l]), bf(ffn2_w_up[l]), bf(ffn2_w_down[l]),
                 row(ffn2_post_g))
    return h.reshape(batch, seq, d)
```

```python
import functools

import jax
import jax.numpy as jnp
from jax import lax
from jax.experimental import pallas as pl
from jax.experimental.pallas import tpu as pltpu

F32 = jnp.float32
BF16 = jnp.bfloat16

CHUNK = 64
HG_HEADS = 16
MLA_HEADS = 16
QK_NOPE = 128
QK_ROPE = 64
V_HEAD = 128
XA_HEADS = 4
ROPE_THETA = 10000.0
FFN_RESIDUAL_WEIGHT = 0.5
EPS = 1e-6
LOG2_E = 1.4426950408889634

LANES = 128
SUBLANES = 8
MIB = 1024 * 1024
VMEM_LIMIT_BYTES = 56 * MIB

HGRN_CHUNK = 128
ATTN_Q_BLOCK = 512
ATTN_HEADS_PER_STEP = 4


def _cparams(*sem):
    return pltpu.CompilerParams(dimension_semantics=sem, vmem_limit_bytes=VMEM_LIMIT_BYTES)


def _rms(x, g):
    return x * lax.rsqrt(jnp.mean(x * x, axis=-1, keepdims=True) + EPS) * g


def _sigmoid(x):
    return 1.0 / (1.0 + jnp.exp(-x))


def _dot(a, b):
    return jnp.dot(a, b, preferred_element_type=F32)


def _dot_nt(a, b):
    return lax.dot_general(a, b, (((1,), (1,)), ((), ())), preferred_element_type=F32)


def _dot_tn(a, b):
    return lax.dot_general(a, b, (((0,), (0,)), ((), ())), preferred_element_type=F32)


def _resident(shape):
    return pl.BlockSpec(shape, lambda *_: (0,) * len(shape), pipeline_mode=pl.Buffered(1))


def _ffn_accumulate(x_ref, pg_ref, wg_ref, wu_ref, wd_ref, h_ref, acc_ref):
    @pl.when(pl.program_id(1) == 0)
    def _():
        h_ref[...] = _rms(x_ref[...], pg_ref[...]).astype(BF16)
        acc_ref[...] = jnp.zeros_like(acc_ref)

    h = h_ref[...]
    g = _dot(h, wg_ref[...])
    u = _dot(h, wu_ref[...])
    a = (g * _sigmoid(g) * u).astype(BF16)
    acc_ref[...] += _dot(a, wd_ref[...])


def _ffn_kernel(x_ref, pg_ref, wg_ref, wu_ref, wd_ref, og_ref, o_ref, h_ref, acc_ref):
    _ffn_accumulate(x_ref, pg_ref, wg_ref, wu_ref, wd_ref, h_ref, acc_ref)

    @pl.when(pl.program_id(1) == pl.num_programs(1) - 1)
    def _():
        o_ref[...] = x_ref[...] + FFN_RESIDUAL_WEIGHT * _rms(acc_ref[...], og_ref[...])


def _ffn_prenorm_kernel(x_ref, pg_ref, wg_ref, wu_ref, wd_ref, og_ref, ng_ref, o_ref, hn_ref,
                        h_ref, acc_ref):
    _ffn_accumulate(x_ref, pg_ref, wg_ref, wu_ref, wd_ref, h_ref, acc_ref)

    @pl.when(pl.program_id(1) == pl.num_programs(1) - 1)
    def _():
        y = x_ref[...] + FFN_RESIDUAL_WEIGHT * _rms(acc_ref[...], og_ref[...])
        o_ref[...] = y
        hn_ref[...] = _rms(y, ng_ref[...]).astype(hn_ref.dtype)


def _ffn(x, pre_g, w_gate, w_up, w_down, post_g, next_g=None, *, tm=512, tf=512):
    t, d = x.shape
    f = w_gate.shape[1]
    row = pl.BlockSpec((1, d), lambda i, j: (0, 0))
    tile = pl.BlockSpec((tm, d), lambda i, j: (i, 0))
    in_specs = [
        tile,
        row,
        pl.BlockSpec((d, tf), lambda i, j: (0, j)),
        pl.BlockSpec((d, tf), lambda i, j: (0, j)),
        pl.BlockSpec((tf, d), lambda i, j: (j, 0)),
        row,
    ]
    args = [x, pre_g, w_gate, w_up, w_down, post_g]
    if next_g is None:
        body, out_specs, out_shape = _ffn_kernel, tile, jax.ShapeDtypeStruct((t, d), F32)
    else:
        body, in_specs, args = _ffn_prenorm_kernel, in_specs + [row], args + [next_g]
        out_specs = [tile, tile]
        out_shape = [jax.ShapeDtypeStruct((t, d), F32), jax.ShapeDtypeStruct((t, d), BF16)]
    return pl.pallas_call(
        body,
        grid=(t // tm, f // tf),
        in_specs=in_specs,
        out_specs=out_specs,
        out_shape=out_shape,
        scratch_shapes=[pltpu.VMEM((tm, d), BF16), pltpu.VMEM((tm, d), F32)],
        compiler_params=_cparams("parallel", "arbitrary"),
        name="ffn",
    )(*args)


def _in_proj_kernel(h_ref, w_ref, wc_ref, u_ref, f_ref, c_ref, *, n_u, n_f):
    j = pl.program_id(1)

    @pl.when(j < n_u)
    def _():
        u_ref[...] = _dot(h_ref[...], w_ref[...]).astype(u_ref.dtype)

    @pl.when((j >= n_u) & (j < n_u + n_f))
    def _():
        f_ref[...] = _dot(h_ref[...], w_ref[...])

    @pl.when(j == n_u + n_f)
    def _():
        c_ref[...] = _dot(h_ref[...], wc_ref[...]).astype(c_ref.dtype)


def _in_proj(h, w_uf, w_c, *, width_u, tm=1024, tn=1024):
    t, d = h.shape
    n_u = width_u // tn
    n_f = (w_uf.shape[1] - width_u) // tn
    n_c = w_c.shape[1]
    return pl.pallas_call(
        functools.partial(_in_proj_kernel, n_u=n_u, n_f=n_f),
        grid=(t // tm, n_u + n_f + 1),
        in_specs=[
            pl.BlockSpec((tm, d), lambda i, j: (i, 0)),
            pl.BlockSpec((d, tn), lambda i, j: (0, jnp.minimum(j, n_u + n_f - 1))),
            _resident(w_c.shape),
        ],
        out_specs=[
            pl.BlockSpec((tm, tn), lambda i, j: (i, jnp.minimum(j, n_u - 1))),
            pl.BlockSpec((tm, tn), lambda i, j: (i, jnp.clip(j - n_u, 0, n_f - 1))),
            pl.BlockSpec((tm, n_c), lambda i, j: (i, 0)),
        ],
        out_shape=[
            jax.ShapeDtypeStruct((t, width_u), BF16),
            jax.ShapeDtypeStruct((t, n_f * tn), F32),
            jax.ShapeDtypeStruct((t, n_c), BF16),
        ],
        compiler_params=_cparams("parallel", "arbitrary"),
        name="in_proj",
    )(h, w_uf, w_c)


def _norm_mm_kernel(x_ref, g_ref, w_ref, o_ref, h_ref):
    @pl.when(pl.program_id(1) == 0)
    def _():
        h_ref[...] = _rms(x_ref[...].astype(F32), g_ref[...]).astype(BF16)

    o_ref[...] = _dot(h_ref[...], w_ref[...]).astype(o_ref.dtype)


def _norm_mm(x, g, w, out_dtype, *, tm, tn, name):
    t, k = x.shape
    n = w.shape[1]
    return pl.pallas_call(
        _norm_mm_kernel,
        grid=(t // tm, n // tn),
        in_specs=[
            pl.BlockSpec((tm, k), lambda i, j: (i, 0)),
            pl.BlockSpec((1, k), lambda i, j: (0, 0)),
            pl.BlockSpec((k, tn), lambda i, j: (0, j)),
        ],
        out_specs=pl.BlockSpec((tm, tn), lambda i, j: (i, j)),
        out_shape=jax.ShapeDtypeStruct((t, n), out_dtype),
        scratch_shapes=[pltpu.VMEM((tm, k), BF16)],
        compiler_params=_cparams("parallel", "arbitrary"),
        name=name,
    )(x, g, w)


def _scan_level_rows(x, m):
    es, xs = [], []
    for lo in range(0, x.shape[0], 2 * m):
        lower, upper = x[lo:lo + m], x[lo + m:lo + 2 * m]
        boundary = lower[m - 1:m, :]
        es += [boundary - lower, upper]
        xs += [lower, upper + boundary]
    return jnp.concatenate(es, axis=0), jnp.concatenate(xs, axis=0)


def _scan_level_sublanes(x3, sub, m):
    if m == 1:
        boundary = jnp.where((sub & 1) == 1, pltpu.roll(x3, 1, 1), x3)
    elif m == 2:
        boundary = jnp.where(sub < 4, x3[:, 1:2, :], x3[:, 5:6, :])
    else:
        boundary = x3[:, 3:4, :]
    upper = (sub & m) != 0
    return jnp.where(upper, x3, boundary - x3), jnp.where(upper, x3 + boundary, x3)


def _hgrn_kernel(q_ref, f_ref, i_ref, og_ref, lb_ref, ng_ref, o_ref, *, chunk):
    seq, kdim = q_ref.shape
    n_chunks = seq // chunk
    tiles = chunk // SUBLANES
    lb = lb_ref[...]
    one_m_lb = 1.0 - lb
    ng = ng_ref[...]
    sub = lax.broadcasted_iota(jnp.int32, (1, SUBLANES, kdim), 1)
    tt = lax.broadcasted_iota(jnp.int32, (chunk, chunk), 0)
    ss = lax.broadcasted_iota(jnp.int32, (chunk, chunk), 1)
    diff = tt ^ ss
    level = jnp.zeros((chunk, chunk), jnp.int32)
    m = 2
    while m < chunk:
        level = level + (diff >= m).astype(jnp.int32)
        m *= 2
    level = jnp.where(tt > ss, level, jnp.where(tt == ss, -1, -2))

    def body(c, st):
        sl = pl.ds(pl.multiple_of(c * chunk, chunk), chunk)
        fr = f_ref[sl, :]
        e = jnp.exp(-jnp.abs(fr))
        r = 1.0 / (1.0 + e)
        er = e * r
        nonneg = fr >= 0.0
        sig = jnp.where(nonneg, r, er)
        sig_neg = jnp.where(nonneg, er, r)
        x = jnp.log(lb + one_m_lb * sig) * LOG2_E
        kk = one_m_lb * sig_neg
        qf = q_ref[sl, :].astype(F32)
        qq = qf * _sigmoid(qf)
        vv = i_ref[sl, :]

        def pair_scores(e2):
            w = jnp.exp2(e2)
            return _dot_nt((qq * w).astype(BF16), (kk * w).astype(BF16))

        x3 = x.reshape(tiles, SUBLANES, kdim)
        a = jnp.where(level == -1, jnp.sum(qq * kk, axis=-1, keepdims=True), 0.0)
        m, lvl = 1, 0
        while m < SUBLANES:
            e3, x3 = _scan_level_sublanes(x3, sub, m)
            a = jnp.where(level == lvl, pair_scores(e3.reshape(chunk, kdim)), a)
            m, lvl = 2 * m, lvl + 1
        x = x3.reshape(chunk, kdim)
        while m < chunk:
            e2, x = _scan_level_rows(x, m)
            p = pair_scores(e2)
            blocks = []
            for lo in range(0, chunk, 2 * m):
                up = slice(lo + m, lo + 2 * m)
                blocks += [a[lo:lo + m], jnp.where(level[up] == lvl, p[up], a[up])]
            a = jnp.concatenate(blocks, axis=0)
            m, lvl = 2 * m, lvl + 1
        b = x

        o = _dot(a.astype(BF16), vv) + _dot_nt((qq * jnp.exp2(b)).astype(BF16), st.astype(BF16))
        b_last = b[chunk - 1:chunk, :]
        kd = (kk * jnp.exp2(b_last - b)).astype(BF16)
        st = jnp.exp2(b_last) * st + _dot_tn(vv, kd)

        og = og_ref[sl, :].astype(F32)
        o_ref[sl, :] = (_rms(o, ng) * (og * _sigmoid(og))).astype(o_ref.dtype)
        return st

    lax.fori_loop(0, n_chunks, body, jnp.zeros((kdim, kdim), F32), unroll=2)


def _hgrn(u, f_raw, lb, norm_g, *, batch, seq):
    t, width = f_raw.shape
    hd = width // HG_HEADS
    u3 = u.reshape(batch, seq, u.shape[1])
    f3 = f_raw.reshape(batch, seq, width)

    def col(group):
        return pl.BlockSpec((None, seq, hd), lambda b, h, g=group: (b, 0, g * HG_HEADS + h))

    out = pl.pallas_call(
        functools.partial(_hgrn_kernel, chunk=HGRN_CHUNK),
        grid=(batch, HG_HEADS),
        in_specs=[
            col(0),
            pl.BlockSpec((None, seq, hd), lambda b, h: (b, 0, h)),
            col(1),
            col(2),
            pl.BlockSpec((1, hd), lambda b, h: (0, h)),
            pl.BlockSpec((1, hd), lambda b, h: (0, h)),
        ],
        out_specs=pl.BlockSpec((None, seq, hd), lambda b, h: (b, 0, h)),
        out_shape=jax.ShapeDtypeStruct((batch, seq, width), BF16),
        compiler_params=_cparams("parallel", "parallel"),
        name="hgrn2",
    )(u3, f3, u3, u3, lb, norm_g)
    return out.reshape(t, width)


def _rotate(pair, cs):
    y = pair * cs
    return y + pltpu.roll(y, QK_ROPE, 1)


def _q_up_kernel(c_ref, g_ref, w_ref, cs_ref, o_ref):
    h = _rms(c_ref[...].astype(F32), g_ref[...]).astype(BF16)
    cs = cs_ref[...]
    head_w = QK_NOPE + 2 * QK_ROPE
    for hd in range(MLA_HEADS):
        lo = hd * head_w
        acc = _dot(h, w_ref[:, lo:lo + head_w])
        o_ref[:, lo:lo + QK_NOPE] = acc[:, :QK_NOPE].astype(o_ref.dtype)
        o_ref[:, lo + QK_NOPE:lo + head_w] = _rotate(acc[:, QK_NOPE:], cs).astype(o_ref.dtype)


def _q_up(cgroup, g, w, cs, *, tm=512):
    t = cgroup.shape[0]
    k = g.shape[1]
    n = w.shape[1]
    return pl.pallas_call(
        _q_up_kernel,
        grid=(t // tm,),
        in_specs=[
            pl.BlockSpec((tm, k), lambda i: (i, 0)),
            _resident((1, k)),
            _resident((k, n)),
            pl.BlockSpec((tm, LANES), lambda i: (i, 0)),
        ],
        out_specs=pl.BlockSpec((tm, n), lambda i: (i, 0)),
        out_shape=jax.ShapeDtypeStruct((t, n), BF16),
        compiler_params=_cparams("parallel"),
        name="mla_q_up",
    )(cgroup, g, w, cs)


def _kv_up_kernel(ckv_ref, kpe_ref, g_ref, w_ref, cs_ref, k_ref, v_ref):
    h = _rms(ckv_ref[...].astype(F32), g_ref[...]).astype(BF16)
    rot = _rotate(kpe_ref[...].astype(F32), cs_ref[...])
    lane = lax.broadcasted_iota(jnp.int32, rot.shape, 1)
    k_pe = jnp.where(lane < QK_ROPE, rot, 0.0).astype(k_ref.dtype)
    head_w = QK_NOPE + 2 * QK_ROPE
    nope_w = MLA_HEADS * QK_NOPE
    k_nope = _dot(h, w_ref[:, :nope_w])
    v = _dot(h, w_ref[:, nope_w:])
    ones_col = jnp.where(lane == 0, 1.0, 0.0).astype(v_ref.dtype)
    for hd in range(MLA_HEADS):
        lo = hd * head_w
        k_ref[:, lo:lo + QK_NOPE] = k_nope[:, hd * QK_NOPE:(hd + 1) * QK_NOPE].astype(k_ref.dtype)
        k_ref[:, lo + QK_NOPE:lo + head_w] = k_pe
        lo = hd * 2 * V_HEAD
        v_ref[:, lo:lo + V_HEAD] = v[:, hd * V_HEAD:(hd + 1) * V_HEAD].astype(v_ref.dtype)
        v_ref[:, lo + V_HEAD:lo + 2 * V_HEAD] = ones_col


def _kv_up(cgroup, g, w, cs, *, tm=512):
    t = cgroup.shape[0]
    k = g.shape[1]
    n_k = MLA_HEADS * (QK_NOPE + 2 * QK_ROPE)
    n_v = MLA_HEADS * 2 * V_HEAD
    return pl.pallas_call(
        _kv_up_kernel,
        grid=(t // tm,),
        in_specs=[
            pl.BlockSpec((tm, k), lambda i: (i, 1)),
            pl.BlockSpec((tm, LANES), lambda i: (i, 2 * k // LANES)),
            _resident((1, k)),
            _resident((k, w.shape[1])),
            pl.BlockSpec((tm, LANES), lambda i: (i, 0)),
        ],
        out_specs=[pl.BlockSpec((tm, n_k), lambda i: (i, 0)),
                   pl.BlockSpec((tm, n_v), lambda i: (i, 0))],
        out_shape=[jax.ShapeDtypeStruct((t, n_k), BF16), jax.ShapeDtypeStruct((t, n_v), BF16)],
        compiler_params=_cparams("parallel"),
        name="mla_kv_up",
    )(cgroup, cgroup, g, w, cs)


def _mla_attn_kernel(q_ref, k_ref, v_ref, o_ref, *, scale, tq, chunk):
    seq = q_ref.shape[0]
    row_chunk = lax.broadcasted_iota(jnp.int32, (tq, tq), 0) // chunk
    col_chunk = lax.broadcasted_iota(jnp.int32, (tq, tq), 1) // chunk
    visible = col_chunk <= row_chunk
    scale2 = scale * LOG2_E
    head_w = QK_NOPE + 2 * QK_ROPE
    for iq in range(seq // tq):
        lo = iq * tq
        for hd in range(q_ref.shape[1] // head_w):
            qk = slice(hd * head_w, (hd + 1) * head_w)
            vc = slice(hd * 2 * V_HEAD, (hd + 1) * 2 * V_HEAD)
            q = q_ref[lo:lo + tq, qk]
            s_diag = jnp.where(visible, _dot_nt(q, k_ref[lo:lo + tq, qk]) * scale2, -jnp.inf)
            m = jnp.max(s_diag, axis=-1, keepdims=True)
            if iq > 0:
                s_past = _dot_nt(q, k_ref[0:lo, qk]) * scale2
                m = jnp.maximum(m, jnp.max(s_past, axis=-1, keepdims=True))
                acc = _dot(jnp.exp2(s_past - m).astype(BF16), v_ref[0:lo, vc])
                acc = acc + _dot(jnp.exp2(s_diag - m).astype(BF16), v_ref[lo:lo + tq, vc])
            else:
                acc = _dot(jnp.exp2(s_diag - m).astype(BF16), v_ref[lo:lo + tq, vc])
            out = acc[:, :V_HEAD] * (1.0 / acc[:, V_HEAD:V_HEAD + 1])
            o_ref[lo:lo + tq, hd * V_HEAD:(hd + 1) * V_HEAD] = out.astype(o_ref.dtype)


def _mla_attn(q, k, v, *, batch, seq):
    t = q.shape[0]
    head_w = ATTN_HEADS_PER_STEP * (QK_NOPE + 2 * QK_ROPE)
    v_w = ATTN_HEADS_PER_STEP * V_HEAD
    q3 = q.reshape(batch, seq, q.shape[1])
    k3 = k.reshape(batch, seq, k.shape[1])
    v3 = v.reshape(batch, seq, v.shape[1])
    out = pl.pallas_call(
        functools.partial(_mla_attn_kernel, scale=(QK_NOPE + QK_ROPE) ** -0.5,
                          tq=ATTN_Q_BLOCK, chunk=CHUNK),
        grid=(batch, MLA_HEADS // ATTN_HEADS_PER_STEP),
        in_specs=[
            pl.BlockSpec((None, seq, head_w), lambda b, h: (b, 0, h)),
            pl.BlockSpec((None, seq, head_w), lambda b, h: (b, 0, h)),
            pl.BlockSpec((None, seq, 2 * v_w), lambda b, h: (b, 0, h)),
        ],
        out_specs=pl.BlockSpec((None, seq, v_w), lambda b, h: (b, 0, h)),
        out_shape=jax.ShapeDtypeStruct((batch, seq, MLA_HEADS * V_HEAD), BF16),
        compiler_params=_cparams("parallel", "parallel"),
        name="mla_attn",
    )(q3, k3, v3)
    return out.reshape(t, MLA_HEADS * V_HEAD)


def _merge_kernel(oa_ref, ob_ref, ga_ref, gb_ref, x_ref, wa_ref, wb_ref, wo_ref, pg_ref, o_ref):
    ya = _dot(oa_ref[...], wa_ref[...])
    yb = _dot(ob_ref[...], wb_ref[...])
    y = _sigmoid(ga_ref[...].astype(F32)) * ya + _sigmoid(gb_ref[...].astype(F32)) * yb
    z = _dot(y.astype(BF16), wo_ref[...])
    o_ref[...] = x_ref[...] + _rms(z, pg_ref[...])


def _merge(o_a, o_b, u, x, w_a, w_b, w_o, post_g, *, gate_col, tm=256):
    t, d = x.shape
    return pl.pallas_call(
        _merge_kernel,
        grid=(t // tm,),
        in_specs=[
            pl.BlockSpec((tm, d), lambda i: (i, 0)),
            pl.BlockSpec((tm, d), lambda i: (i, 0)),
            pl.BlockSpec((tm, d), lambda i: (i, gate_col)),
            pl.BlockSpec((tm, d), lambda i: (i, gate_col + 1)),
            pl.BlockSpec((tm, d), lambda i: (i, 0)),
            _resident(w_a.shape),
            _resident(w_b.shape),
            _resident(w_o.shape),
            _resident((1, d)),
        ],
        out_specs=pl.BlockSpec((tm, d), lambda i: (i, 0)),
        out_shape=jax.ShapeDtypeStruct((t, d), F32),
        compiler_params=_cparams("parallel"),
        name="merge",
    )(o_a, o_b, u, u, x, w_a, w_b, w_o, post_g)


def _xattn_kernel(x_ref, pg_ref, wq_ref, kv_ref, wo_ref, og_ref, o_ref, *, scale):
    x = x_ref[...]
    q = _dot(_rms(x, pg_ref[...]).astype(BF16), wq_ref[...]).astype(BF16)
    width = wq_ref.shape[1]
    hd = width // XA_HEADS
    heads = []
    for h in range(XA_HEADS):
        s = _dot_nt(q[:, h * hd:(h + 1) * hd], kv_ref[:, h * hd:(h + 1) * hd]) * scale
        p = jnp.exp(s - jnp.max(s, axis=-1, keepdims=True))
        denom = jnp.sum(p, axis=-1, keepdims=True)
        o = _dot(p.astype(BF16), kv_ref[:, width + h * hd:width + (h + 1) * hd])
        heads.append((o * (1.0 / denom)).astype(BF16))
    z = _dot(jnp.concatenate(heads, axis=-1), wo_ref[...])
    o_ref[...] = x + _rms(z, og_ref[...])


def _xattn(x, pre_g, w_q, kv_mem, w_o, post_g, *, batch, seq, n_mem, tm=512):
    t, d = x.shape
    width = w_q.shape[1]
    x3 = x.reshape(batch, seq, d)
    kv3 = kv_mem.reshape(batch, n_mem, 2 * width)
    out = pl.pallas_call(
        functools.partial(_xattn_kernel, scale=(width // XA_HEADS) ** -0.5),
        grid=(batch, seq // tm),
        in_specs=[
            pl.BlockSpec((None, tm, d), lambda b, i: (b, i, 0)),
            _resident((1, d)),
            _resident(w_q.shape),
            pl.BlockSpec((None, n_mem, 2 * width), lambda b, i: (b, 0, 0)),
            _resident(w_o.shape),
            _resident((1, d)),
        ],
        out_specs=pl.BlockSpec((None, tm, d), lambda b, i: (b, i, 0)),
        out_shape=jax.ShapeDtypeStruct((batch, seq, d), F32),
        compiler_params=_cparams("parallel", "parallel"),
        name="xattn",
    )(x3, pre_g, w_q, kv3, w_o, post_g)
    return out.reshape(t, d)


def _rotate_half_columns(w):
    half = w.shape[-1] // 2
    return jnp.concatenate([-w[..., half:], w[..., :half]], axis=-1)


def kernel(x, mem, positions, hgrn_lb_logits, ffn1_pre_g, ffn1_w_gate, ffn1_w_up, ffn1_w_down, ffn1_post_g, mix_pre_g, w_in, hg_norm_g, mla_q_norm_g, mla_w_q_up, mla_kv_norm_g, mla_w_kv_up, w_branch_a, w_branch_b, w_out, mix_post_g, xa_pre_g, xa_mem_g, xa_w_q, xa_w_k, xa_w_v, xa_w_o, xa_post_g, ffn2_pre_g, ffn2_w_gate, ffn2_w_up, ffn2_w_down, ffn2_post_g):
    batch, seq, d = x.shape
    n_mem = mem.shape[1]
    depth = ffn1_w_gate.shape[0]
    t = batch * seq
    q_lora = mla_q_norm_g.shape[1]
    kv_lora = mla_kv_norm_g.shape[1]
    hg_k = hgrn_lb_logits.shape[1]
    hg_v = hg_norm_g.shape[1]
    assert q_lora == kv_lora, "cgroup column blocks assume equal latent widths"
    assert hg_k == hg_v == d and d // HG_HEADS == LANES

    inv_freq = 1.0 / (ROPE_THETA ** (jnp.arange(0, QK_ROPE, 2, dtype=F32) / QK_ROPE))
    ang = positions.astype(F32)[..., None] * inv_freq
    cos, sin = jnp.cos(ang), jnp.sin(ang)
    cs = jnp.concatenate([cos, cos, sin, sin], axis=-1).reshape(t, 2 * QK_ROPE)
    lower_bounds = jnp.cumsum(jax.nn.softmax(hgrn_lb_logits.astype(F32), axis=0), axis=0)

    o_q, o_f, o_i, o_og = 0, hg_k, 2 * hg_k, 2 * hg_k + hg_v
    o_cq = o_og + hg_v
    o_ckv = o_cq + q_lora
    o_kpe = o_ckv + kv_lora
    o_ga = o_kpe + QK_ROPE
    o_gb = o_ga + d

    h = x.reshape(t, d)
    mem2 = mem.reshape(batch * n_mem, d)
    for l in range(depth):
        row = lambda g: g[l][None, :]
        bf = lambda w: w.astype(BF16)

        h, h_mix = _ffn(h, row(ffn1_pre_g), bf(ffn1_w_gate[l]), bf(ffn1_w_up[l]), bf(ffn1_w_down[l]),
                        row(ffn1_post_g), row(mix_pre_g))

        wi = w_in[l]
        width_u = hg_k + 2 * hg_v + 2 * d
        w_uf = bf(jnp.concatenate([wi[:, o_q:o_q + hg_k], wi[:, o_i:o_i + hg_v], wi[:, o_og:o_og + hg_v],
                                   wi[:, o_ga:o_ga + d], wi[:, o_gb:o_gb + d],
                                   wi[:, o_f:o_f + hg_k]], axis=1))
        w_kpe = wi[:, o_kpe:o_kpe + QK_ROPE]
        w_c = bf(jnp.concatenate([wi[:, o_cq:o_cq + q_lora], wi[:, o_ckv:o_ckv + kv_lora],
                                  w_kpe, _rotate_half_columns(w_kpe)], axis=1))
        u, f_raw, cgroup = _in_proj(h_mix, w_uf, w_c, width_u=width_u)

        o_a = _hgrn(u, f_raw, lower_bounds[l][None, :], row(hg_norm_g), batch=batch, seq=seq)

        wq = mla_w_q_up[l].reshape(q_lora, MLA_HEADS, QK_NOPE + QK_ROPE)
        wq_pe = wq[..., QK_NOPE:]
        wq_full = bf(jnp.concatenate([wq[..., :QK_NOPE], wq_pe, _rotate_half_columns(wq_pe)], axis=-1)
                     .reshape(q_lora, -1))
        wkv = mla_w_kv_up[l].reshape(kv_lora, MLA_HEADS, QK_NOPE + V_HEAD)
        wkv_full = bf(jnp.concatenate([wkv[..., :QK_NOPE].reshape(kv_lora, -1),
                                       wkv[..., QK_NOPE:].reshape(kv_lora, -1)], axis=1))
        q_full = _q_up(cgroup, row(mla_q_norm_g), wq_full, cs)
        k_full, v = _kv_up(cgroup, row(mla_kv_norm_g), wkv_full, cs)
        o_b = _mla_attn(q_full, k_full, v, batch=batch, seq=seq)

        h = _merge(o_a, o_b, u, h, bf(w_branch_a[l]), bf(w_branch_b[l]), bf(w_out[l]),
                   row(mix_post_g), gate_col=(hg_k + 2 * hg_v) // d)

        w_kv_mem = bf(jnp.concatenate([xa_w_k[l], xa_w_v[l]], axis=1))
        kv_mem = _norm_mm(mem2, row(xa_mem_g), w_kv_mem, BF16, tm=512, tn=w_kv_mem.shape[1],
                          name="mem_kv")
        h = _xattn(h, row(xa_pre_g), bf(xa_w_q[l]), kv_mem, bf(xa_w_o[l]), row(xa_post_g),
                   batch=batch, seq=seq, n_mem=n_mem)

        h = _ffn(h, row(ffn2_pre_g), bf(ffn2_w_gate[l]), bf(ffn2_w_up[l]), bf(ffn2_w_down[l]),
                 row(ffn2_post_g))
    return h.reshape(batch, seq, d)
```

```python
import functools

import jax
import jax.numpy as jnp
from jax import lax
from jax.experimental import pallas as pl
from jax.experimental.pallas import tpu as pltpu

F32 = jnp.float32
BF16 = jnp.bfloat16

CHUNK = 64
HG_HEADS = 16
MLA_HEADS = 16
QK_NOPE = 128
QK_ROPE = 64
V_HEAD = 128
XA_HEADS = 4
ROPE_THETA = 10000.0
FFN_RESIDUAL_WEIGHT = 0.5
EPS = 1e-6
LOG2_E = 1.4426950408889634

LANES = 128
SUBLANES = 8
MIB = 1024 * 1024
VMEM_LIMIT_BYTES = 56 * MIB

IN_PROJ_TILE = 1024
HGRN_CHUNK = 128
ATTN_Q_BLOCK = 512
ATTN_HEADS_PER_STEP = 4


def _cparams(*sem):
    return pltpu.CompilerParams(dimension_semantics=sem, vmem_limit_bytes=VMEM_LIMIT_BYTES)


def _rms(x, g):
    return x * lax.rsqrt(jnp.mean(x * x, axis=-1, keepdims=True) + EPS) * g


def _sigmoid(x):
    return 1.0 / (1.0 + jnp.exp(-x))


def _dot(a, b):
    return jnp.dot(a, b, preferred_element_type=F32)


def _dot_nt(a, b):
    return lax.dot_general(a, b, (((1,), (1,)), ((), ())), preferred_element_type=F32)


def _dot_tn(a, b):
    return lax.dot_general(a, b, (((0,), (0,)), ((), ())), preferred_element_type=F32)


def _resident(shape):
    return pl.BlockSpec(shape, lambda *_: (0,) * len(shape), pipeline_mode=pl.Buffered(1))


def _ffn_accumulate(x_ref, pg_ref, wg_ref, wu_ref, wd_ref, h_ref, acc_ref):
    @pl.when(pl.program_id(1) == 0)
    def _():
        h_ref[...] = _rms(x_ref[...], pg_ref[...]).astype(BF16)
        acc_ref[...] = jnp.zeros_like(acc_ref)

    h = h_ref[...]
    g = _dot(h, wg_ref[...])
    u = _dot(h, wu_ref[...])
    a = (g * _sigmoid(g) * u).astype(BF16)
    acc_ref[...] += _dot(a, wd_ref[...])


def _ffn_kernel(x_ref, pg_ref, wg_ref, wu_ref, wd_ref, og_ref, o_ref, h_ref, acc_ref):
    _ffn_accumulate(x_ref, pg_ref, wg_ref, wu_ref, wd_ref, h_ref, acc_ref)

    @pl.when(pl.program_id(1) == pl.num_programs(1) - 1)
    def _():
        o_ref[...] = x_ref[...] + FFN_RESIDUAL_WEIGHT * _rms(acc_ref[...], og_ref[...])


def _ffn_prenorm_kernel(x_ref, pg_ref, wg_ref, wu_ref, wd_ref, og_ref, ng_ref, o_ref, hn_ref,
                        h_ref, acc_ref):
    _ffn_accumulate(x_ref, pg_ref, wg_ref, wu_ref, wd_ref, h_ref, acc_ref)

    @pl.when(pl.program_id(1) == pl.num_programs(1) - 1)
    def _():
        y = x_ref[...] + FFN_RESIDUAL_WEIGHT * _rms(acc_ref[...], og_ref[...])
        o_ref[...] = y
        hn_ref[...] = _rms(y, ng_ref[...]).astype(hn_ref.dtype)


def _ffn(x, pre_g, w_gate, w_up, w_down, post_g, next_g=None, *, tm=512, tf=512):
    t, d = x.shape
    f = w_gate.shape[1]
    row = pl.BlockSpec((1, d), lambda i, j: (0, 0))
    tile = pl.BlockSpec((tm, d), lambda i, j: (i, 0))
    in_specs = [
        tile,
        row,
        pl.BlockSpec((d, tf), lambda i, j: (0, j)),
        pl.BlockSpec((d, tf), lambda i, j: (0, j)),
        pl.BlockSpec((tf, d), lambda i, j: (j, 0)),
        row,
    ]
    args = [x, pre_g, w_gate, w_up, w_down, post_g]
    if next_g is None:
        body, out_specs, out_shape = _ffn_kernel, tile, jax.ShapeDtypeStruct((t, d), F32)
    else:
        body, in_specs, args = _ffn_prenorm_kernel, in_specs + [row], args + [next_g]
        out_specs = [tile, tile]
        out_shape = [jax.ShapeDtypeStruct((t, d), F32), jax.ShapeDtypeStruct((t, d), BF16)]
    return pl.pallas_call(
        body,
        grid=(t // tm, f // tf),
        in_specs=in_specs,
        out_specs=out_specs,
        out_shape=out_shape,
        scratch_shapes=[pltpu.VMEM((tm, d), BF16), pltpu.VMEM((tm, d), F32)],
        compiler_params=_cparams("parallel", "arbitrary"),
        name="ffn",
    )(*args)


def _by_step(table, j):
    out = table[0]
    for step, value in enumerate(table[1:], start=1):
        out = jnp.where(j == step, value, out)
    return out


def _any_step(steps, j):
    hit = j == steps[0]
    for step in steps[1:]:
        hit = hit | (j == step)
    return hit


def _in_proj_kernel(h_ref, w_ref, wg_ref, wk_ref, u_ref, f_ref, c_ref, *, plan):
    j = pl.program_id(1)

    @pl.when(_any_step(plan["u_steps"], j))
    def _():
        u_ref[...] = _dot(h_ref[...], w_ref[...]).astype(u_ref.dtype)

    @pl.when(_any_step(plan["gate_steps"], j))
    def _():
        u_ref[...] = _dot(h_ref[...], wg_ref[...]).astype(u_ref.dtype)

    @pl.when(_any_step(plan["f_steps"], j))
    def _():
        f_ref[...] = _dot(h_ref[...], w_ref[...])

    @pl.when(j == plan["c_step"])
    def _():
        h = h_ref[...]
        tn = w_ref.shape[1]
        c_ref[:, :tn] = _dot(h, w_ref[...]).astype(c_ref.dtype)
        r = _dot(h, wk_ref[...])
        lane = lax.broadcasted_iota(jnp.int32, r.shape, 1)
        half = QK_ROPE // 2
        pair = jnp.where(lane < QK_ROPE, r,
                         jnp.where(lane < QK_ROPE + half, -pltpu.roll(r, half, 1),
                                   pltpu.roll(r, QK_ROPE + half, 1)))
        c_ref[:, tn:] = pair.astype(c_ref.dtype)


def _in_proj(h, w_all, w_gates, *, groups, c_tile, kpe_col, tm=1024, tn=1024):
    t, d = h.shape
    n_gate = w_gates.shape[1] // tn
    w_tab, u_tab, f_tab, u_steps, f_steps = [], [], [], [], []
    n_u = n_f = 0
    for dest, first, count in groups:
        for k in range(count):
            step = len(w_tab)
            w_tab.append(first + k)
            if dest == "u":
                u_steps.append(step)
                n_u += 1
            else:
                f_steps.append(step)
                n_f += 1
            u_tab.append(max(n_u - 1, 0))
            f_tab.append(max(n_f - 1, 0))
    c_step = len(w_tab)
    w_tab.append(c_tile)
    u_tab.append(max(n_u - 1, 0))
    f_tab.append(max(n_f - 1, 0))
    gate_steps = list(range(c_step + 1, c_step + 1 + n_gate))
    n_steps = c_step + 1 + n_gate
    w_tab += [c_tile] * n_gate
    u_tab += [n_u + k for k in range(n_gate)]
    f_tab += [max(n_f - 1, 0)] * n_gate
    g_tab = [0] * (c_step + 1) + list(range(n_gate))
    plan = dict(u_steps=u_steps, f_steps=f_steps, gate_steps=gate_steps, c_step=c_step)
    return pl.pallas_call(
        functools.partial(_in_proj_kernel, plan=plan),
        grid=(t // tm, n_steps),
        in_specs=[
            pl.BlockSpec((tm, d), lambda i, j: (i, 0)),
            pl.BlockSpec((d, tn), lambda i, j: (0, _by_step(w_tab, j))),
            pl.BlockSpec((d, tn), lambda i, j: (0, _by_step(g_tab, j))),
            pl.BlockSpec((d, LANES), lambda i, j: (0, kpe_col // LANES), pipeline_mode=pl.Buffered(1)),
        ],
        out_specs=[
            pl.BlockSpec((tm, tn), lambda i, j: (i, _by_step(u_tab, j))),
            pl.BlockSpec((tm, tn), lambda i, j: (i, _by_step(f_tab, j))),
            pl.BlockSpec((tm, tn + LANES), lambda i, j: (i, 0)),
        ],
        out_shape=[
            jax.ShapeDtypeStruct((t, (n_u + n_gate) * tn), BF16),
            jax.ShapeDtypeStruct((t, n_f * tn), F32),
            jax.ShapeDtypeStruct((t, tn + LANES), BF16),
        ],
        compiler_params=_cparams("parallel", "arbitrary"),
        name="in_proj",
    )(h, w_all, w_gates, w_all)


def _norm_mm_kernel(x_ref, g_ref, w_ref, o_ref, h_ref):
    @pl.when(pl.program_id(1) == 0)
    def _():
        h_ref[...] = _rms(x_ref[...].astype(F32), g_ref[...]).astype(BF16)

    o_ref[...] = _dot(h_ref[...], w_ref[...]).astype(o_ref.dtype)


def _norm_mm(x, g, w, out_dtype, *, tm, tn, name):
    t, k = x.shape
    n = w.shape[1]
    return pl.pallas_call(
        _norm_mm_kernel,
        grid=(t // tm, n // tn),
        in_specs=[
            pl.BlockSpec((tm, k), lambda i, j: (i, 0)),
            pl.BlockSpec((1, k), lambda i, j: (0, 0)),
            pl.BlockSpec((k, tn), lambda i, j: (0, j)),
        ],
        out_specs=pl.BlockSpec((tm, tn), lambda i, j: (i, j)),
        out_shape=jax.ShapeDtypeStruct((t, n), out_dtype),
        scratch_shapes=[pltpu.VMEM((tm, k), BF16)],
        compiler_params=_cparams("parallel", "arbitrary"),
        name=name,
    )(x, g, w)


def _scan_level_rows(x, m):
    es, xs = [], []
    for lo in range(0, x.shape[0], 2 * m):
        lower, upper = x[lo:lo + m], x[lo + m:lo + 2 * m]
        boundary = lower[m - 1:m, :]
        es += [boundary - lower, upper]
        xs += [lower, upper + boundary]
    return jnp.concatenate(es, axis=0), jnp.concatenate(xs, axis=0)


def _scan_level_sublanes(x3, sub, m):
    if m == 1:
        boundary = jnp.where((sub & 1) == 1, pltpu.roll(x3, 1, 1), x3)
    elif m == 2:
        boundary = jnp.where(sub < 4, x3[:, 1:2, :], x3[:, 5:6, :])
    else:
        boundary = x3[:, 3:4, :]
    upper = (sub & m) != 0
    return jnp.where(upper, x3, boundary - x3), jnp.where(upper, x3 + boundary, x3)


def _hgrn_kernel(q_ref, f_ref, i_ref, og_ref, lb_ref, ng_ref, o_ref, *, chunk):
    seq, kdim = q_ref.shape
    n_chunks = seq // chunk
    tiles = chunk // SUBLANES
    lb = lb_ref[...]
    one_m_lb = 1.0 - lb
    ng = ng_ref[...]
    sub = lax.broadcasted_iota(jnp.int32, (1, SUBLANES, kdim), 1)
    tt = lax.broadcasted_iota(jnp.int32, (chunk, chunk), 0)
    ss = lax.broadcasted_iota(jnp.int32, (chunk, chunk), 1)
    diff = tt ^ ss
    level = jnp.zeros((chunk, chunk), jnp.int32)
    m = 2
    while m < chunk:
        level = level + (diff >= m).astype(jnp.int32)
        m *= 2
    level = jnp.where(tt > ss, level, jnp.where(tt == ss, -1, -2))

    def body(c, st):
        sl = pl.ds(pl.multiple_of(c * chunk, chunk), chunk)
        fr = f_ref[sl, :]
        e = jnp.exp(-jnp.abs(fr))
        r = 1.0 / (1.0 + e)
        er = e * r
        nonneg = fr >= 0.0
        sig = jnp.where(nonneg, r, er)
        sig_neg = jnp.where(nonneg, er, r)
        x = jnp.log(lb + one_m_lb * sig) * LOG2_E
        kk = one_m_lb * sig_neg
        qf = q_ref[sl, :].astype(F32)
        qq = qf * _sigmoid(qf)
        vv = i_ref[sl, :]
        qq16, kk16 = qq.astype(BF16), kk.astype(BF16)

        def pair_scores(e2):
            w = jnp.exp2(e2).astype(BF16)
            return _dot_nt(qq16 * w, kk16 * w)

        x3 = x.reshape(tiles, SUBLANES, kdim)
        a = jnp.where(level == -1, jnp.sum(qq * kk, axis=-1, keepdims=True), 0.0)
        m, lvl = 1, 0
        while m < SUBLANES:
            e3, x3 = _scan_level_sublanes(x3, sub, m)
            a = jnp.where(level == lvl, pair_scores(e3.reshape(chunk, kdim)), a)
            m, lvl = 2 * m, lvl + 1
        x = x3.reshape(chunk, kdim)
        while m < chunk:
            e2, x = _scan_level_rows(x, m)
            p = pair_scores(e2)
            blocks = []
            for lo in range(0, chunk, 2 * m):
                up = slice(lo + m, lo + 2 * m)
                blocks += [a[lo:lo + m], jnp.where(level[up] == lvl, p[up], a[up])]
            a = jnp.concatenate(blocks, axis=0)
            m, lvl = 2 * m, lvl + 1
        b = x

        o = _dot(a.astype(BF16), vv) + _dot_nt((qq * jnp.exp2(b)).astype(BF16), st.astype(BF16))
        b_last = b[chunk - 1:chunk, :]
        kd = (kk * jnp.exp2(b_last - b)).astype(BF16)
        st = jnp.exp2(b_last) * st + _dot_tn(vv, kd)

        og = og_ref[sl, :].astype(F32)
        o_ref[sl, :] = (_rms(o, ng) * (og * _sigmoid(og))).astype(o_ref.dtype)
        return st

    lax.fori_loop(0, n_chunks, body, jnp.zeros((kdim, kdim), F32), unroll=8)


def _hgrn(u, f_raw, lb, norm_g, *, batch, seq):
    t, width = f_raw.shape
    hd = width // HG_HEADS
    u3 = u.reshape(batch, seq, u.shape[1])
    f3 = f_raw.reshape(batch, seq, width)

    def col(group):
        return pl.BlockSpec((None, seq, hd), lambda b, h, g=group: (b, 0, g * HG_HEADS + h))

    out = pl.pallas_call(
        functools.partial(_hgrn_kernel, chunk=HGRN_CHUNK),
        grid=(batch, HG_HEADS),
        in_specs=[
            col(0),
            pl.BlockSpec((None, seq, hd), lambda b, h: (b, 0, h)),
            col(1),
            col(2),
            pl.BlockSpec((1, hd), lambda b, h: (0, h)),
            pl.BlockSpec((1, hd), lambda b, h: (0, h)),
        ],
        out_specs=pl.BlockSpec((None, seq, hd), lambda b, h: (b, 0, h)),
        out_shape=jax.ShapeDtypeStruct((batch, seq, width), BF16),
        compiler_params=_cparams("parallel", "parallel"),
        name="hgrn2",
    )(u3, f3, u3, u3, lb, norm_g)
    return out.reshape(t, width)


def _rotate(pair, cs):
    y = pair * cs
    return y + pltpu.roll(y, QK_ROPE, 1)


def _q_up_kernel(c_ref, g_ref, w_ref, cs_ref, o_ref):
    h = _rms(c_ref[...].astype(F32), g_ref[...]).astype(BF16)
    cs = cs_ref[...]
    head_w = QK_NOPE + 2 * QK_ROPE
    for hd in range(MLA_HEADS):
        lo = hd * head_w
        acc = _dot(h, w_ref[:, lo:lo + head_w])
        o_ref[:, lo:lo + QK_NOPE] = acc[:, :QK_NOPE].astype(o_ref.dtype)
        o_ref[:, lo + QK_NOPE:lo + head_w] = _rotate(acc[:, QK_NOPE:], cs).astype(o_ref.dtype)


def _q_up(cgroup, g, w, cs, *, tm=512):
    t = cgroup.shape[0]
    k = g.shape[1]
    n = w.shape[1]
    return pl.pallas_call(
        _q_up_kernel,
        grid=(t // tm,),
        in_specs=[
            pl.BlockSpec((tm, k), lambda i: (i, 0)),
            _resident((1, k)),
            _resident((k, n)),
            pl.BlockSpec((tm, LANES), lambda i: (i, 0)),
        ],
        out_specs=pl.BlockSpec((tm, n), lambda i: (i, 0)),
        out_shape=jax.ShapeDtypeStruct((t, n), BF16),
        compiler_params=_cparams("parallel"),
        name="mla_q_up",
    )(cgroup, g, w, cs)


def _kv_up_kernel(ckv_ref, kpe_ref, g_ref, w_ref, cs_ref, k_ref, v_ref):
    h = _rms(ckv_ref[...].astype(F32), g_ref[...]).astype(BF16)
    rot = _rotate(kpe_ref[...].astype(F32), cs_ref[...])
    lane = lax.broadcasted_iota(jnp.int32, rot.shape, 1)
    k_pe = jnp.where(lane < QK_ROPE, rot, 0.0).astype(k_ref.dtype)
    head_w = QK_NOPE + 2 * QK_ROPE
    nope_w = MLA_HEADS * QK_NOPE
    k_nope = _dot(h, w_ref[:, :nope_w])
    v = _dot(h, w_ref[:, nope_w:])
    ones_col = jnp.where(lane == 0, 1.0, 0.0).astype(v_ref.dtype)
    for hd in range(MLA_HEADS):
        lo = hd * head_w
        k_ref[:, lo:lo + QK_NOPE] = k_nope[:, hd * QK_NOPE:(hd + 1) * QK_NOPE].astype(k_ref.dtype)
        k_ref[:, lo + QK_NOPE:lo + head_w] = k_pe
        lo = hd * 2 * V_HEAD
        v_ref[:, lo:lo + V_HEAD] = v[:, hd * V_HEAD:(hd + 1) * V_HEAD].astype(v_ref.dtype)
        v_ref[:, lo + V_HEAD:lo + 2 * V_HEAD] = ones_col


def _kv_up(cgroup, g, w, cs, *, tm=512):
    t = cgroup.shape[0]
    k = g.shape[1]
    n_k = MLA_HEADS * (QK_NOPE + 2 * QK_ROPE)
    n_v = MLA_HEADS * 2 * V_HEAD
    return pl.pallas_call(
        _kv_up_kernel,
        grid=(t // tm,),
        in_specs=[
            pl.BlockSpec((tm, k), lambda i: (i, 1)),
            pl.BlockSpec((tm, LANES), lambda i: (i, 2 * k // LANES)),
            _resident((1, k)),
            _resident((k, w.shape[1])),
            pl.BlockSpec((tm, LANES), lambda i: (i, 0)),
        ],
        out_specs=[pl.BlockSpec((tm, n_k), lambda i: (i, 0)),
                   pl.BlockSpec((tm, n_v), lambda i: (i, 0))],
        out_shape=[jax.ShapeDtypeStruct((t, n_k), BF16), jax.ShapeDtypeStruct((t, n_v), BF16)],
        compiler_params=_cparams("parallel"),
        name="mla_kv_up",
    )(cgroup, cgroup, g, w, cs)


def _mla_attn_kernel(q_ref, k_ref, v_ref, o_ref, *, scale, tq, chunk):
    seq = q_ref.shape[0]
    row_chunk = lax.broadcasted_iota(jnp.int32, (tq, tq), 0) // chunk
    col_chunk = lax.broadcasted_iota(jnp.int32, (tq, tq), 1) // chunk
    visible = col_chunk <= row_chunk
    scale2 = scale * LOG2_E
    head_w = QK_NOPE + 2 * QK_ROPE
    for iq in range(seq // tq):
        lo = iq * tq
        for hd in range(q_ref.shape[1] // head_w):
            qk = slice(hd * head_w, (hd + 1) * head_w)
            vc = slice(hd * 2 * V_HEAD, (hd + 1) * 2 * V_HEAD)
            q = q_ref[lo:lo + tq, qk]
            s_diag = jnp.where(visible, _dot_nt(q, k_ref[lo:lo + tq, qk]) * scale2, -jnp.inf)
            m = jnp.max(s_diag, axis=-1, keepdims=True)
            if iq > 0:
                s_past = _dot_nt(q, k_ref[0:lo, qk]) * scale2
                m = jnp.maximum(m, jnp.max(s_past, axis=-1, keepdims=True))
                acc = _dot(jnp.exp2(s_past - m).astype(BF16), v_ref[0:lo, vc])
                acc = acc + _dot(jnp.exp2(s_diag - m).astype(BF16), v_ref[lo:lo + tq, vc])
            else:
                acc = _dot(jnp.exp2(s_diag - m).astype(BF16), v_ref[lo:lo + tq, vc])
            out = acc[:, :V_HEAD] * (1.0 / acc[:, V_HEAD:V_HEAD + 1])
            o_ref[lo:lo + tq, hd * V_HEAD:(hd + 1) * V_HEAD] = out.astype(o_ref.dtype)


def _mla_attn(q, k, v, *, batch, seq):
    t = q.shape[0]
    head_w = ATTN_HEADS_PER_STEP * (QK_NOPE + 2 * QK_ROPE)
    v_w = ATTN_HEADS_PER_STEP * V_HEAD
    q3 = q.reshape(batch, seq, q.shape[1])
    k3 = k.reshape(batch, seq, k.shape[1])
    v3 = v.reshape(batch, seq, v.shape[1])
    out = pl.pallas_call(
        functools.partial(_mla_attn_kernel, scale=(QK_NOPE + QK_ROPE) ** -0.5,
                          tq=ATTN_Q_BLOCK, chunk=CHUNK),
        grid=(batch, MLA_HEADS // ATTN_HEADS_PER_STEP),
        in_specs=[
            pl.BlockSpec((None, seq, head_w), lambda b, h: (b, 0, h)),
            pl.BlockSpec((None, seq, head_w), lambda b, h: (b, 0, h)),
            pl.BlockSpec((None, seq, 2 * v_w), lambda b, h: (b, 0, h)),
        ],
        out_specs=pl.BlockSpec((None, seq, v_w), lambda b, h: (b, 0, h)),
        out_shape=jax.ShapeDtypeStruct((batch, seq, MLA_HEADS * V_HEAD), BF16),
        compiler_params=_cparams("parallel", "parallel"),
        name="mla_attn",
    )(q3, k3, v3)
    return out.reshape(t, MLA_HEADS * V_HEAD)


def _merge_kernel(oa_ref, ob_ref, ga_ref, gb_ref, x_ref, wa_ref, wb_ref, wo_ref, pg_ref, o_ref):
    ya = _dot(oa_ref[...], wa_ref[...])
    yb = _dot(ob_ref[...], wb_ref[...])
    y = _sigmoid(ga_ref[...].astype(F32)) * ya + _sigmoid(gb_ref[...].astype(F32)) * yb
    z = _dot(y.astype(BF16), wo_ref[...])
    o_ref[...] = x_ref[...] + _rms(z, pg_ref[...])


def _merge(o_a, o_b, u, x, w_a, w_b, w_o, post_g, *, gate_col, tm=256):
    t, d = x.shape
    return pl.pallas_call(
        _merge_kernel,
        grid=(t // tm,),
        in_specs=[
            pl.BlockSpec((tm, d), lambda i: (i, 0)),
            pl.BlockSpec((tm, d), lambda i: (i, 0)),
            pl.BlockSpec((tm, d), lambda i: (i, gate_col)),
            pl.BlockSpec((tm, d), lambda i: (i, gate_col + 1)),
            pl.BlockSpec((tm, d), lambda i: (i, 0)),
            _resident(w_a.shape),
            _resident(w_b.shape),
            _resident(w_o.shape),
            _resident((1, d)),
        ],
        out_specs=pl.BlockSpec((tm, d), lambda i: (i, 0)),
        out_shape=jax.ShapeDtypeStruct((t, d), F32),
        compiler_params=_cparams("parallel"),
        name="merge",
    )(o_a, o_b, u, u, x, w_a, w_b, w_o, post_g)


def _xattn_kernel(x_ref, pg_ref, wq_ref, kv_ref, wo_ref, og_ref, o_ref, *, scale):
    x = x_ref[...]
    q = _dot(_rms(x, pg_ref[...]).astype(BF16), wq_ref[...]).astype(BF16)
    width = wq_ref.shape[1]
    hd = width // XA_HEADS
    heads = []
    for h in range(XA_HEADS):
        s = _dot_nt(q[:, h * hd:(h + 1) * hd], kv_ref[:, h * hd:(h + 1) * hd]) * scale
        p = jnp.exp(s - jnp.max(s, axis=-1, keepdims=True))
        denom = jnp.sum(p, axis=-1, keepdims=True)
        o = _dot(p.astype(BF16), kv_ref[:, width + h * hd:width + (h + 1) * hd])
        heads.append((o * (1.0 / denom)).astype(BF16))
    z = _dot(jnp.concatenate(heads, axis=-1), wo_ref[...])
    o_ref[...] = x + _rms(z, og_ref[...])


def _xattn(x, pre_g, w_q, kv_mem, w_o, post_g, *, batch, seq, n_mem, tm=512):
    t, d = x.shape
    width = w_q.shape[1]
    x3 = x.reshape(batch, seq, d)
    kv3 = kv_mem.reshape(batch, n_mem, 2 * width)
    out = pl.pallas_call(
        functools.partial(_xattn_kernel, scale=(width // XA_HEADS) ** -0.5),
        grid=(batch, seq // tm),
        in_specs=[
            pl.BlockSpec((None, tm, d), lambda b, i: (b, i, 0)),
            _resident((1, d)),
            _resident(w_q.shape),
            pl.BlockSpec((None, n_mem, 2 * width), lambda b, i: (b, 0, 0)),
            _resident(w_o.shape),
            _resident((1, d)),
        ],
        out_specs=pl.BlockSpec((None, tm, d), lambda b, i: (b, i, 0)),
        out_shape=jax.ShapeDtypeStruct((batch, seq, d), F32),
        compiler_params=_cparams("parallel", "parallel"),
        name="xattn",
    )(x3, pre_g, w_q, kv3, w_o, post_g)
    return out.reshape(t, d)


def _rotate_half_columns(w):
    half = w.shape[-1] // 2
    return jnp.concatenate([-w[..., half:], w[..., :half]], axis=-1)


def kernel(x, mem, positions, hgrn_lb_logits, ffn1_pre_g, ffn1_w_gate, ffn1_w_up, ffn1_w_down, ffn1_post_g, mix_pre_g, w_in, hg_norm_g, mla_q_norm_g, mla_w_q_up, mla_kv_norm_g, mla_w_kv_up, w_branch_a, w_branch_b, w_out, mix_post_g, xa_pre_g, xa_mem_g, xa_w_q, xa_w_k, xa_w_v, xa_w_o, xa_post_g, ffn2_pre_g, ffn2_w_gate, ffn2_w_up, ffn2_w_down, ffn2_post_g):
    batch, seq, d = x.shape
    n_mem = mem.shape[1]
    depth = ffn1_w_gate.shape[0]
    t = batch * seq
    q_lora = mla_q_norm_g.shape[1]
    kv_lora = mla_kv_norm_g.shape[1]
    hg_k = hgrn_lb_logits.shape[1]
    hg_v = hg_norm_g.shape[1]
    assert q_lora == kv_lora, "cgroup column blocks assume equal latent widths"
    assert hg_k == hg_v == d and d // HG_HEADS == LANES

    inv_freq = 1.0 / (ROPE_THETA ** (jnp.arange(0, QK_ROPE, 2, dtype=F32) / QK_ROPE))
    ang = positions.astype(F32).reshape(t, 1) * jnp.tile(inv_freq, 4)
    cs = jnp.where(jnp.arange(2 * QK_ROPE) < QK_ROPE, jnp.cos(ang), jnp.sin(ang))
    lower_bounds = jnp.cumsum(jax.nn.softmax(hgrn_lb_logits.astype(F32), axis=0), axis=0)

    o_q, o_f, o_i, o_og = 0, hg_k, 2 * hg_k, 2 * hg_k + hg_v
    o_cq = o_og + hg_v
    o_ckv = o_cq + q_lora
    o_kpe = o_ckv + kv_lora
    o_ga = o_kpe + QK_ROPE
    o_gb = o_ga + d

    h = x.reshape(t, d)
    mem2 = mem.reshape(batch * n_mem, d)
    for l in range(depth):
        row = lambda g: g[l][None, :]
        bf = lambda w: w.astype(BF16)

        h, h_mix = _ffn(h, row(ffn1_pre_g), bf(ffn1_w_gate[l]), bf(ffn1_w_up[l]), bf(ffn1_w_down[l]),
                        row(ffn1_post_g), row(mix_pre_g))

        w_all = bf(w_in[l])
        w_gates = w_all[:, o_ga:o_ga + 2 * d]
        tn = IN_PROJ_TILE
        assert q_lora + kv_lora == tn and o_cq % tn == 0 and o_kpe % LANES == 0
        groups = (("u", o_q // tn, hg_k // tn), ("f", o_f // tn, hg_k // tn),
                  ("u", o_i // tn, 2 * hg_v // tn))
        u, f_raw, cgroup = _in_proj(h_mix, w_all, w_gates, groups=groups, c_tile=o_cq // tn,
                                    kpe_col=o_kpe, tn=tn)

        o_a = _hgrn(u, f_raw, lower_bounds[l][None, :], row(hg_norm_g), batch=batch, seq=seq)

        wq = mla_w_q_up[l].reshape(q_lora, MLA_HEADS, QK_NOPE + QK_ROPE)
        wq_pe = wq[..., QK_NOPE:]
        wq_full = bf(jnp.concatenate([wq[..., :QK_NOPE], wq_pe, _rotate_half_columns(wq_pe)], axis=-1)
                     .reshape(q_lora, -1))
        wkv = mla_w_kv_up[l].reshape(kv_lora, MLA_HEADS, QK_NOPE + V_HEAD)
        wkv_full = bf(jnp.concatenate([wkv[..., :QK_NOPE].reshape(kv_lora, -1),
                                       wkv[..., QK_NOPE:].reshape(kv_lora, -1)], axis=1))
        q_full = _q_up(cgroup, row(mla_q_norm_g), wq_full, cs)
        k_full, v = _kv_up(cgroup, row(mla_kv_norm_g), wkv_full, cs)
        o_b = _mla_attn(q_full, k_full, v, batch=batch, seq=seq)

        h = _merge(o_a, o_b, u, h, bf(w_branch_a[l]), bf(w_branch_b[l]), bf(w_out[l]),
                   row(mix_post_g), gate_col=(hg_k + 2 * hg_v) // d)

        w_kv_mem = bf(jnp.concatenate([xa_w_k[l], xa_w_v[l]], axis=1))
        kv_mem = _norm_mm(mem2, row(xa_mem_g), w_kv_mem, BF16, tm=512, tn=w_kv_mem.shape[1],
                          name="mem_kv")
        h = _xattn(h, row(xa_pre_g), bf(xa_w_q[l]), kv_mem, bf(xa_w_o[l]), row(xa_post_g),
                   batch=batch, seq=seq, n_mem=n_mem)

        h = _ffn(h, row(ffn2_pre_g), bf(ffn2_w_gate[l]), bf(ffn2_w_up[l]), bf(ffn2_w_down[l]),
                 row(ffn2_post_g))
    return h.reshape(batch, seq, d)
```

```python
import functools

import jax
import jax.numpy as jnp
from jax import lax
from jax.experimental import pallas as pl
from jax.experimental.pallas import tpu as pltpu

F32 = jnp.float32
BF16 = jnp.bfloat16

CHUNK = 64
HG_HEADS = 16
MLA_HEADS = 16
QK_NOPE = 128
QK_ROPE = 64
V_HEAD = 128
XA_HEADS = 4
ROPE_THETA = 10000.0
FFN_RESIDUAL_WEIGHT = 0.5
EPS = 1e-6
LOG2_E = 1.4426950408889634

LANES = 128
SUBLANES = 8
BF16_ROWS = 16
MIB = 1024 * 1024
VMEM_LIMIT_BYTES = 56 * MIB

IN_PROJ_TILE = 1024
HGRN_CHUNK = 128
ATTN_Q_BLOCK = 512
ATTN_HEADS_PER_STEP = 4


def _cparams(*sem):
    return pltpu.CompilerParams(dimension_semantics=sem, vmem_limit_bytes=VMEM_LIMIT_BYTES)


def _rms(x, g):
    return x * lax.rsqrt(jnp.mean(x * x, axis=-1, keepdims=True) + EPS) * g


def _sigmoid(x):
    return 1.0 / (1.0 + jnp.exp(-x))


def _dot(a, b):
    return jnp.dot(a, b, preferred_element_type=F32)


def _dot_nt(a, b):
    return lax.dot_general(a, b, (((1,), (1,)), ((), ())), preferred_element_type=F32)


def _dot_tn(a, b):
    return lax.dot_general(a, b, (((0,), (0,)), ((), ())), preferred_element_type=F32)


def _resident(shape):
    return pl.BlockSpec(shape, lambda *_: (0,) * len(shape), pipeline_mode=pl.Buffered(1))


def _ffn_accumulate(x_ref, pg_ref, wg_ref, wu_ref, wd_ref, h_ref, acc_ref):
    @pl.when(pl.program_id(1) == 0)
    def _():
        h_ref[...] = _rms(x_ref[...], pg_ref[...]).astype(BF16)
        acc_ref[...] = jnp.zeros_like(acc_ref)

    h = h_ref[...]
    g = _dot(h, wg_ref[...])
    u = _dot(h, wu_ref[...])
    a = (g * _sigmoid(g) * u).astype(BF16)
    acc_ref[...] += _dot(a, wd_ref[...])


def _ffn_kernel(x_ref, pg_ref, wg_ref, wu_ref, wd_ref, og_ref, o_ref, h_ref, acc_ref):
    _ffn_accumulate(x_ref, pg_ref, wg_ref, wu_ref, wd_ref, h_ref, acc_ref)

    @pl.when(pl.program_id(1) == pl.num_programs(1) - 1)
    def _():
        o_ref[...] = x_ref[...] + FFN_RESIDUAL_WEIGHT * _rms(acc_ref[...], og_ref[...])


def _ffn_first_kernel(x_ref, pg_ref, wg_ref, wu_ref, wd_ref, og_ref, ng_ref, cg_ref, cu_ref, cd_ref,
                      o_ref, hn_ref, cg16_ref, cu16_ref, cd16_ref, h_ref, acc_ref):
    _ffn_accumulate(x_ref, pg_ref, wg_ref, wu_ref, wd_ref, h_ref, acc_ref)
    cg16_ref[...] = cg_ref[...].astype(cg16_ref.dtype)
    cu16_ref[...] = cu_ref[...].astype(cu16_ref.dtype)
    cd16_ref[...] = cd_ref[...].astype(cd16_ref.dtype)

    @pl.when(pl.program_id(1) == pl.num_programs(1) - 1)
    def _():
        y = x_ref[...] + FFN_RESIDUAL_WEIGHT * _rms(acc_ref[...], og_ref[...])
        o_ref[...] = y
        hn_ref[...] = _rms(y, ng_ref[...]).astype(hn_ref.dtype)


def _ffn(x, pre_g, w_gate, w_up, w_down, post_g, next_g=None, cast_weights=None, *, tm=512, tf=512):
    t, d = x.shape
    f = w_gate.shape[1]
    n_i, n_j = t // tm, f // tf
    row = pl.BlockSpec((1, d), lambda i, j: (0, 0))
    tile = pl.BlockSpec((tm, d), lambda i, j: (i, 0))
    in_specs = [
        tile,
        row,
        pl.BlockSpec((d, tf), lambda i, j: (0, j)),
        pl.BlockSpec((d, tf), lambda i, j: (0, j)),
        pl.BlockSpec((tf, d), lambda i, j: (j, 0)),
        row,
    ]
    args = [x, pre_g, w_gate, w_up, w_down, post_g]
    if next_g is None:
        body, out_specs, out_shape = _ffn_kernel, tile, jax.ShapeDtypeStruct((t, d), F32)
    else:
        up_rows, down_rows = d // n_i, f // (n_i * n_j)
        assert d % n_i == 0 and f % (n_i * n_j) == 0
        assert up_rows % BF16_ROWS == 0 and down_rows % BF16_ROWS == 0
        up_slice = pl.BlockSpec((up_rows, tf), lambda i, j: (i, j))
        down_slice = pl.BlockSpec((down_rows, d), lambda i, j: (i * n_j + j, 0))
        body = _ffn_first_kernel
        in_specs = in_specs + [row, up_slice, up_slice, down_slice]
        args = args + [next_g, *cast_weights]
        out_specs = [tile, tile, up_slice, up_slice, down_slice]
        out_shape = [jax.ShapeDtypeStruct((t, d), F32), jax.ShapeDtypeStruct((t, d), BF16),
                     jax.ShapeDtypeStruct((d, f), BF16), jax.ShapeDtypeStruct((d, f), BF16),
                     jax.ShapeDtypeStruct((f, d), BF16)]
    return pl.pallas_call(
        body,
        grid=(t // tm, f // tf),
        in_specs=in_specs,
        out_specs=out_specs,
        out_shape=out_shape,
        scratch_shapes=[pltpu.VMEM((tm, d), BF16), pltpu.VMEM((tm, d), F32)],
        compiler_params=_cparams("parallel", "arbitrary"),
        name="ffn",
    )(*args)


def _by_step(table, j):
    out = table[0]
    for step, value in enumerate(table[1:], start=1):
        out = jnp.where(j == step, value, out)
    return out


def _any_step(steps, j):
    hit = j == steps[0]
    for step in steps[1:]:
        hit = hit | (j == step)
    return hit


def _in_proj_kernel(h_ref, w_ref, wg_ref, wk_ref, u_ref, f_ref, c_ref, *, plan):
    j = pl.program_id(1)

    @pl.when(_any_step(plan["u_steps"], j))
    def _():
        u_ref[...] = _dot(h_ref[...], w_ref[...].astype(BF16)).astype(u_ref.dtype)

    @pl.when(_any_step(plan["gate_steps"], j))
    def _():
        u_ref[...] = _dot(h_ref[...], wg_ref[...]).astype(u_ref.dtype)

    @pl.when(_any_step(plan["f_steps"], j))
    def _():
        f_ref[...] = _dot(h_ref[...], w_ref[...].astype(BF16))

    @pl.when(j == plan["c_step"])
    def _():
        h = h_ref[...]
        tn = w_ref.shape[1]
        c_ref[:, :tn] = _dot(h, w_ref[...].astype(BF16)).astype(c_ref.dtype)
        r = _dot(h, wk_ref[...].astype(BF16))
        lane = lax.broadcasted_iota(jnp.int32, r.shape, 1)
        half = QK_ROPE // 2
        pair = jnp.where(lane < QK_ROPE, r,
                         jnp.where(lane < QK_ROPE + half, -pltpu.roll(r, half, 1),
                                   pltpu.roll(r, QK_ROPE + half, 1)))
        c_ref[:, tn:] = pair.astype(c_ref.dtype)


def _in_proj(h, w_all, w_gates, *, groups, c_tile, kpe_col, tm=1024, tn=1024):
    t, d = h.shape
    n_gate = w_gates.shape[1] // tn
    w_tab, u_tab, f_tab, u_steps, f_steps = [], [], [], [], []
    n_u = n_f = 0
    for dest, first, count in groups:
        for k in range(count):
            step = len(w_tab)
            w_tab.append(first + k)
            if dest == "u":
                u_steps.append(step)
                n_u += 1
            else:
                f_steps.append(step)
                n_f += 1
            u_tab.append(max(n_u - 1, 0))
            f_tab.append(max(n_f - 1, 0))
    c_step = len(w_tab)
    w_tab.append(c_tile)
    u_tab.append(max(n_u - 1, 0))
    f_tab.append(max(n_f - 1, 0))
    gate_steps = list(range(c_step + 1, c_step + 1 + n_gate))
    n_steps = c_step + 1 + n_gate
    w_tab += [c_tile] * n_gate
    u_tab += [n_u + k for k in range(n_gate)]
    f_tab += [max(n_f - 1, 0)] * n_gate
    g_tab = [0] * (c_step + 1) + list(range(n_gate))
    plan = dict(u_steps=u_steps, f_steps=f_steps, gate_steps=gate_steps, c_step=c_step)
    return pl.pallas_call(
        functools.partial(_in_proj_kernel, plan=plan),
        grid=(t // tm, n_steps),
        in_specs=[
            pl.BlockSpec((tm, d), lambda i, j: (i, 0)),
            pl.BlockSpec((d, tn), lambda i, j: (0, _by_step(w_tab, j))),
            pl.BlockSpec((d, tn), lambda i, j: (0, _by_step(g_tab, j))),
            pl.BlockSpec((d, LANES), lambda i, j: (0, kpe_col // LANES), pipeline_mode=pl.Buffered(1)),
        ],
        out_specs=[
            pl.BlockSpec((tm, tn), lambda i, j: (i, _by_step(u_tab, j))),
            pl.BlockSpec((tm, tn), lambda i, j: (i, _by_step(f_tab, j))),
            pl.BlockSpec((tm, tn + LANES), lambda i, j: (i, 0)),
        ],
        out_shape=[
            jax.ShapeDtypeStruct((t, (n_u + n_gate) * tn), BF16),
            jax.ShapeDtypeStruct((t, n_f * tn), F32),
            jax.ShapeDtypeStruct((t, tn + LANES), BF16),
        ],
        compiler_params=_cparams("parallel", "arbitrary"),
        name="in_proj",
    )(h, w_all, w_gates, w_all)


def _norm_mm_kernel(x_ref, g_ref, w_ref, o_ref, h_ref):
    @pl.when(pl.program_id(1) == 0)
    def _():
        h_ref[...] = _rms(x_ref[...].astype(F32), g_ref[...]).astype(BF16)

    o_ref[...] = _dot(h_ref[...], w_ref[...]).astype(o_ref.dtype)


def _norm_mm(x, g, w, out_dtype, *, tm, tn, name):
    t, k = x.shape
    n = w.shape[1]
    return pl.pallas_call(
        _norm_mm_kernel,
        grid=(t // tm, n // tn),
        in_specs=[
            pl.BlockSpec((tm, k), lambda i, j: (i, 0)),
            pl.BlockSpec((1, k), lambda i, j: (0, 0)),
            pl.BlockSpec((k, tn), lambda i, j: (0, j)),
        ],
        out_specs=pl.BlockSpec((tm, tn), lambda i, j: (i, j)),
        out_shape=jax.ShapeDtypeStruct((t, n), out_dtype),
        scratch_shapes=[pltpu.VMEM((tm, k), BF16)],
        compiler_params=_cparams("parallel", "arbitrary"),
        name=name,
    )(x, g, w)


def _scan_level_rows(x, m):
    es, xs = [], []
    for lo in range(0, x.shape[0], 2 * m):
        lower, upper = x[lo:lo + m], x[lo + m:lo + 2 * m]
        boundary = lower[m - 1:m, :]
        es += [boundary - lower, upper]
        xs += [lower, upper + boundary]
    return jnp.concatenate(es, axis=0), jnp.concatenate(xs, axis=0)


def _scan_level_sublanes(x3, sub, m):
    if m == 1:
        boundary = jnp.where((sub & 1) == 1, pltpu.roll(x3, 1, 1), x3)
    elif m == 2:
        boundary = jnp.where(sub < 4, x3[:, 1:2, :], x3[:, 5:6, :])
    else:
        boundary = x3[:, 3:4, :]
    upper = (sub & m) != 0
    return jnp.where(upper, x3, boundary - x3), jnp.where(upper, x3 + boundary, x3)


def _hgrn_kernel(q_ref, f_ref, i_ref, og_ref, lb_ref, ng_ref, o_ref, *, chunk):
    seq, kdim = q_ref.shape
    n_chunks = seq // chunk
    tiles = chunk // SUBLANES
    lb = lb_ref[...]
    one_m_lb = 1.0 - lb
    ng = ng_ref[...]
    sub = lax.broadcasted_iota(jnp.int32, (1, SUBLANES, kdim), 1)
    tt = lax.broadcasted_iota(jnp.int32, (chunk, chunk), 0)
    ss = lax.broadcasted_iota(jnp.int32, (chunk, chunk), 1)
    diff = tt ^ ss
    level = jnp.zeros((chunk, chunk), jnp.int32)
    m = 2
    while m < chunk:
        level = level + (diff >= m).astype(jnp.int32)
        m *= 2
    level = jnp.where(tt > ss, level, jnp.where(tt == ss, -1, -2))

    def body(c, st):
        sl = pl.ds(pl.multiple_of(c * chunk, chunk), chunk)
        fr = f_ref[sl, :]
        e = jnp.exp(-jnp.abs(fr))
        r = 1.0 / (1.0 + e)
        er = e * r
        nonneg = fr >= 0.0
        sig = jnp.where(nonneg, r, er)
        sig_neg = jnp.where(nonneg, er, r)
        x = jnp.log(lb + one_m_lb * sig) * LOG2_E
        kk = one_m_lb * sig_neg
        qf = q_ref[sl, :].astype(F32)
        qq = qf * _sigmoid(qf)
        vv = i_ref[sl, :]
        qq16, kk16 = qq.astype(BF16), kk.astype(BF16)

        def pair_scores(e2):
            w = jnp.exp2(e2).astype(BF16)
            return _dot_nt(qq16 * w, kk16 * w)

        x3 = x.reshape(tiles, SUBLANES, kdim)
        a = jnp.where(level == -1, jnp.sum(qq * kk, axis=-1, keepdims=True), 0.0)
        m, lvl = 1, 0
        while m < SUBLANES:
            e3, x3 = _scan_level_sublanes(x3, sub, m)
            a = jnp.where(level == lvl, pair_scores(e3.reshape(chunk, kdim)), a)
            m, lvl = 2 * m, lvl + 1
        x = x3.reshape(chunk, kdim)
        while m < chunk:
            e2, x = _scan_level_rows(x, m)
            p = pair_scores(e2)
            blocks = []
            for lo in range(0, chunk, 2 * m):
                up = slice(lo + m, lo + 2 * m)
                blocks += [a[lo:lo + m], jnp.where(level[up] == lvl, p[up], a[up])]
            a = jnp.concatenate(blocks, axis=0)
            m, lvl = 2 * m, lvl + 1
        b = x

        o = _dot(a.astype(BF16), vv) + _dot_nt((qq * jnp.exp2(b)).astype(BF16), st.astype(BF16))
        b_last = b[chunk - 1:chunk, :]
        kd = (kk * jnp.exp2(b_last - b)).astype(BF16)
        st = jnp.exp2(b_last) * st + _dot_tn(vv, kd)

        og = og_ref[sl, :].astype(F32)
        o_ref[sl, :] = (_rms(o, ng) * (og * _sigmoid(og))).astype(o_ref.dtype)
        return st

    lax.fori_loop(0, n_chunks, body, jnp.zeros((kdim, kdim), F32), unroll=8)


def _hgrn(u, f_raw, lb, norm_g, *, batch, seq):
    t, width = f_raw.shape
    hd = width // HG_HEADS
    u3 = u.reshape(batch, seq, u.shape[1])
    f3 = f_raw.reshape(batch, seq, width)

    def col(group):
        return pl.BlockSpec((None, seq, hd), lambda b, h, g=group: (b, 0, g * HG_HEADS + h))

    out = pl.pallas_call(
        functools.partial(_hgrn_kernel, chunk=HGRN_CHUNK),
        grid=(batch, HG_HEADS),
        in_specs=[
            col(0),
            pl.BlockSpec((None, seq, hd), lambda b, h: (b, 0, h)),
            col(1),
            col(2),
            pl.BlockSpec((1, hd), lambda b, h: (0, h)),
            pl.BlockSpec((1, hd), lambda b, h: (0, h)),
        ],
        out_specs=pl.BlockSpec((None, seq, hd), lambda b, h: (b, 0, h)),
        out_shape=jax.ShapeDtypeStruct((batch, seq, width), BF16),
        compiler_params=_cparams("parallel", "parallel"),
        name="hgrn2",
    )(u3, f3, u3, u3, lb, norm_g)
    return out.reshape(t, width)


def _rotate(pair, cs):
    y = pair * cs
    return y + pltpu.roll(y, QK_ROPE, 1)


def _q_up_kernel(c_ref, g_ref, w_ref, cs_ref, o_ref):
    h = _rms(c_ref[...].astype(F32), g_ref[...]).astype(BF16)
    cs = cs_ref[...]
    head_w = QK_NOPE + 2 * QK_ROPE
    for hd in range(MLA_HEADS):
        lo = hd * head_w
        acc = _dot(h, w_ref[:, lo:lo + head_w])
        o_ref[:, lo:lo + QK_NOPE] = acc[:, :QK_NOPE].astype(o_ref.dtype)
        o_ref[:, lo + QK_NOPE:lo + head_w] = _rotate(acc[:, QK_NOPE:], cs).astype(o_ref.dtype)


def _q_up(cgroup, g, w, cs, *, tm=512):
    t = cgroup.shape[0]
    k = g.shape[1]
    n = w.shape[1]
    return pl.pallas_call(
        _q_up_kernel,
        grid=(t // tm,),
        in_specs=[
            pl.BlockSpec((tm, k), lambda i: (i, 0)),
            _resident((1, k)),
            _resident((k, n)),
            pl.BlockSpec((tm, LANES), lambda i: (i, 0)),
        ],
        out_specs=pl.BlockSpec((tm, n), lambda i: (i, 0)),
        out_shape=jax.ShapeDtypeStruct((t, n), BF16),
        compiler_params=_cparams("parallel"),
        name="mla_q_up",
    )(cgroup, g, w, cs)


def _kv_up_kernel(ckv_ref, kpe_ref, g_ref, w_ref, cs_ref, k_ref, v_ref):
    h = _rms(ckv_ref[...].astype(F32), g_ref[...]).astype(BF16)
    rot = _rotate(kpe_ref[...].astype(F32), cs_ref[...])
    lane = lax.broadcasted_iota(jnp.int32, rot.shape, 1)
    k_pe = jnp.where(lane < QK_ROPE, rot, 0.0).astype(k_ref.dtype)
    head_w = QK_NOPE + 2 * QK_ROPE
    nope_w = MLA_HEADS * QK_NOPE
    k_nope = _dot(h, w_ref[:, :nope_w])
    v = _dot(h, w_ref[:, nope_w:])
    ones_col = jnp.where(lane == 0, 1.0, 0.0).astype(v_ref.dtype)
    for hd in range(MLA_HEADS):
        lo = hd * head_w
        k_ref[:, lo:lo + QK_NOPE] = k_nope[:, hd * QK_NOPE:(hd + 1) * QK_NOPE].astype(k_ref.dtype)
        k_ref[:, lo + QK_NOPE:lo + head_w] = k_pe
        lo = hd * 2 * V_HEAD
        v_ref[:, lo:lo + V_HEAD] = v[:, hd * V_HEAD:(hd + 1) * V_HEAD].astype(v_ref.dtype)
        v_ref[:, lo + V_HEAD:lo + 2 * V_HEAD] = ones_col


def _kv_up(cgroup, g, w, cs, *, tm=512):
    t = cgroup.shape[0]
    k = g.shape[1]
    n_k = MLA_HEADS * (QK_NOPE + 2 * QK_ROPE)
    n_v = MLA_HEADS * 2 * V_HEAD
    return pl.pallas_call(
        _kv_up_kernel,
        grid=(t // tm,),
        in_specs=[
            pl.BlockSpec((tm, k), lambda i: (i, 1)),
            pl.BlockSpec((tm, LANES), lambda i: (i, 2 * k // LANES)),
            _resident((1, k)),
            _resident((k, w.shape[1])),
            pl.BlockSpec((tm, LANES), lambda i: (i, 0)),
        ],
        out_specs=[pl.BlockSpec((tm, n_k), lambda i: (i, 0)),
                   pl.BlockSpec((tm, n_v), lambda i: (i, 0))],
        out_shape=[jax.ShapeDtypeStruct((t, n_k), BF16), jax.ShapeDtypeStruct((t, n_v), BF16)],
        compiler_params=_cparams("parallel"),
        name="mla_kv_up",
    )(cgroup, cgroup, g, w, cs)


def _mla_attn_kernel(q_ref, k_ref, v_ref, o_ref, *, scale, tq, chunk):
    seq = q_ref.shape[0]
    row_chunk = lax.broadcasted_iota(jnp.int32, (tq, tq), 0) // chunk
    col_chunk = lax.broadcasted_iota(jnp.int32, (tq, tq), 1) // chunk
    visible = col_chunk <= row_chunk
    scale2 = scale * LOG2_E
    head_w = QK_NOPE + 2 * QK_ROPE
    for iq in range(seq // tq):
        lo = iq * tq
        for hd in range(q_ref.shape[1] // head_w):
            qk = slice(hd * head_w, (hd + 1) * head_w)
            vc = slice(hd * 2 * V_HEAD, (hd + 1) * 2 * V_HEAD)
            q = q_ref[lo:lo + tq, qk]
            s_diag = jnp.where(visible, _dot_nt(q, k_ref[lo:lo + tq, qk]) * scale2, -jnp.inf)
            m = jnp.max(s_diag, axis=-1, keepdims=True)
            if iq > 0:
                s_past = _dot_nt(q, k_ref[0:lo, qk]) * scale2
                m = jnp.maximum(m, jnp.max(s_past, axis=-1, keepdims=True))
                acc = _dot(jnp.exp2(s_past - m).astype(BF16), v_ref[0:lo, vc])
                acc = acc + _dot(jnp.exp2(s_diag - m).astype(BF16), v_ref[lo:lo + tq, vc])
            else:
                acc = _dot(jnp.exp2(s_diag - m).astype(BF16), v_ref[lo:lo + tq, vc])
            out = acc[:, :V_HEAD] * (1.0 / acc[:, V_HEAD:V_HEAD + 1])
            o_ref[lo:lo + tq, hd * V_HEAD:(hd + 1) * V_HEAD] = out.astype(o_ref.dtype)


def _mla_attn(q, k, v, *, batch, seq):
    t = q.shape[0]
    head_w = ATTN_HEADS_PER_STEP * (QK_NOPE + 2 * QK_ROPE)
    v_w = ATTN_HEADS_PER_STEP * V_HEAD
    q3 = q.reshape(batch, seq, q.shape[1])
    k3 = k.reshape(batch, seq, k.shape[1])
    v3 = v.reshape(batch, seq, v.shape[1])
    out = pl.pallas_call(
        functools.partial(_mla_attn_kernel, scale=(QK_NOPE + QK_ROPE) ** -0.5,
                          tq=ATTN_Q_BLOCK, chunk=CHUNK),
        grid=(batch, MLA_HEADS // ATTN_HEADS_PER_STEP),
        in_specs=[
            pl.BlockSpec((None, seq, head_w), lambda b, h: (b, 0, h)),
            pl.BlockSpec((None, seq, head_w), lambda b, h: (b, 0, h)),
            pl.BlockSpec((None, seq, 2 * v_w), lambda b, h: (b, 0, h)),
        ],
        out_specs=pl.BlockSpec((None, seq, v_w), lambda b, h: (b, 0, h)),
        out_shape=jax.ShapeDtypeStruct((batch, seq, MLA_HEADS * V_HEAD), BF16),
        compiler_params=_cparams("parallel", "parallel"),
        name="mla_attn",
    )(q3, k3, v3)
    return out.reshape(t, MLA_HEADS * V_HEAD)


def _merge_kernel(oa_ref, ob_ref, ga_ref, gb_ref, x_ref, wa_ref, wb_ref, wo_ref, pg_ref, o_ref):
    ya = _dot(oa_ref[...], wa_ref[...])
    yb = _dot(ob_ref[...], wb_ref[...])
    y = _sigmoid(ga_ref[...].astype(F32)) * ya + _sigmoid(gb_ref[...].astype(F32)) * yb
    z = _dot(y.astype(BF16), wo_ref[...])
    o_ref[...] = x_ref[...] + _rms(z, pg_ref[...])


def _merge(o_a, o_b, u, x, w_a, w_b, w_o, post_g, *, gate_col, tm=256):
    t, d = x.shape
    return pl.pallas_call(
        _merge_kernel,
        grid=(t // tm,),
        in_specs=[
            pl.BlockSpec((tm, d), lambda i: (i, 0)),
            pl.BlockSpec((tm, d), lambda i: (i, 0)),
            pl.BlockSpec((tm, d), lambda i: (i, gate_col)),
            pl.BlockSpec((tm, d), lambda i: (i, gate_col + 1)),
            pl.BlockSpec((tm, d), lambda i: (i, 0)),
            _resident(w_a.shape),
            _resident(w_b.shape),
            _resident(w_o.shape),
            _resident((1, d)),
        ],
        out_specs=pl.BlockSpec((tm, d), lambda i: (i, 0)),
        out_shape=jax.ShapeDtypeStruct((t, d), F32),
        compiler_params=_cparams("parallel"),
        name="merge",
    )(o_a, o_b, u, u, x, w_a, w_b, w_o, post_g)


def _xattn_kernel(x_ref, pg_ref, wq_ref, kv_ref, wo_ref, og_ref, o_ref, *, scale):
    x = x_ref[...]
    q = _dot(_rms(x, pg_ref[...]).astype(BF16), wq_ref[...]).astype(BF16)
    width = wq_ref.shape[1]
    hd = width // XA_HEADS
    heads = []
    for h in range(XA_HEADS):
        s = _dot_nt(q[:, h * hd:(h + 1) * hd], kv_ref[:, h * hd:(h + 1) * hd]) * scale
        p = jnp.exp(s - jnp.max(s, axis=-1, keepdims=True))
        denom = jnp.sum(p, axis=-1, keepdims=True)
        o = _dot(p.astype(BF16), kv_ref[:, width + h * hd:width + (h + 1) * hd])
        heads.append((o * (1.0 / denom)).astype(BF16))
    z = _dot(jnp.concatenate(heads, axis=-1), wo_ref[...])
    o_ref[...] = x + _rms(z, og_ref[...])


def _xattn(x, pre_g, w_q, kv_mem, w_o, post_g, *, batch, seq, n_mem, tm=512):
    t, d = x.shape
    width = w_q.shape[1]
    x3 = x.reshape(batch, seq, d)
    kv3 = kv_mem.reshape(batch, n_mem, 2 * width)
    out = pl.pallas_call(
        functools.partial(_xattn_kernel, scale=(width // XA_HEADS) ** -0.5),
        grid=(batch, seq // tm),
        in_specs=[
            pl.BlockSpec((None, tm, d), lambda b, i: (b, i, 0)),
            _resident((1, d)),
            _resident(w_q.shape),
            pl.BlockSpec((None, n_mem, 2 * width), lambda b, i: (b, 0, 0)),
            _resident(w_o.shape),
            _resident((1, d)),
        ],
        out_specs=pl.BlockSpec((None, tm, d), lambda b, i: (b, i, 0)),
        out_shape=jax.ShapeDtypeStruct((batch, seq, d), F32),
        compiler_params=_cparams("parallel", "parallel"),
        name="xattn",
    )(x3, pre_g, w_q, kv3, w_o, post_g)
    return out.reshape(t, d)


def _rotate_half_columns(w):
    half = w.shape[-1] // 2
    return jnp.concatenate([-w[..., half:], w[..., :half]], axis=-1)


def kernel(x, mem, positions, hgrn_lb_logits, ffn1_pre_g, ffn1_w_gate, ffn1_w_up, ffn1_w_down, ffn1_post_g, mix_pre_g, w_in, hg_norm_g, mla_q_norm_g, mla_w_q_up, mla_kv_norm_g, mla_w_kv_up, w_branch_a, w_branch_b, w_out, mix_post_g, xa_pre_g, xa_mem_g, xa_w_q, xa_w_k, xa_w_v, xa_w_o, xa_post_g, ffn2_pre_g, ffn2_w_gate, ffn2_w_up, ffn2_w_down, ffn2_post_g):
    batch, seq, d = x.shape
    n_mem = mem.shape[1]
    depth = ffn1_w_gate.shape[0]
    t = batch * seq
    q_lora = mla_q_norm_g.shape[1]
    kv_lora = mla_kv_norm_g.shape[1]
    hg_k = hgrn_lb_logits.shape[1]
    hg_v = hg_norm_g.shape[1]
    assert q_lora == kv_lora, "cgroup column blocks assume equal latent widths"
    assert hg_k == hg_v == d and d // HG_HEADS == LANES

    inv_freq = 1.0 / (ROPE_THETA ** (jnp.arange(0, QK_ROPE, 2, dtype=F32) / QK_ROPE))
    ang = positions.astype(F32).reshape(t, 1) * jnp.tile(inv_freq, 4)
    cs = jnp.where(jnp.arange(2 * QK_ROPE) < QK_ROPE, jnp.cos(ang), jnp.sin(ang))
    lower_bounds = jnp.cumsum(jax.nn.softmax(hgrn_lb_logits.astype(F32), axis=0), axis=0)

    o_q, o_f, o_i, o_og = 0, hg_k, 2 * hg_k, 2 * hg_k + hg_v
    o_cq = o_og + hg_v
    o_ckv = o_cq + q_lora
    o_kpe = o_ckv + kv_lora
    o_ga = o_kpe + QK_ROPE
    o_gb = o_ga + d

    h = x.reshape(t, d)
    mem2 = mem.reshape(batch * n_mem, d)
    for l in range(depth):
        row = lambda g: g[l][None, :]
        bf = lambda w: w.astype(BF16)

        h, h_mix, w2_gate, w2_up, w2_down = _ffn(
            h, row(ffn1_pre_g), bf(ffn1_w_gate[l]), bf(ffn1_w_up[l]), bf(ffn1_w_down[l]),
            row(ffn1_post_g), row(mix_pre_g), (ffn2_w_gate[l], ffn2_w_up[l], ffn2_w_down[l]))

        w_all = w_in[l]
        w_gates = bf(w_all[:, o_ga:o_ga + 2 * d])
        tn = IN_PROJ_TILE
        assert q_lora + kv_lora == tn and o_cq % tn == 0 and o_kpe % LANES == 0
        groups = (("u", o_q // tn, hg_k // tn), ("f", o_f // tn, hg_k // tn),
                  ("u", o_i // tn, 2 * hg_v // tn))
        u, f_raw, cgroup = _in_proj(h_mix, w_all, w_gates, groups=groups, c_tile=o_cq // tn,
                                    kpe_col=o_kpe, tn=tn)

        o_a = _hgrn(u, f_raw, lower_bounds[l][None, :], row(hg_norm_g), batch=batch, seq=seq)

        wq = mla_w_q_up[l].reshape(q_lora, MLA_HEADS, QK_NOPE + QK_ROPE)
        wq_pe = wq[..., QK_NOPE:]
        wq_full = bf(jnp.concatenate([wq[..., :QK_NOPE], wq_pe, _rotate_half_columns(wq_pe)], axis=-1)
                     .reshape(q_lora, -1))
        wkv = mla_w_kv_up[l].reshape(kv_lora, MLA_HEADS, QK_NOPE + V_HEAD)
        wkv_full = bf(jnp.concatenate([wkv[..., :QK_NOPE].reshape(kv_lora, -1),
                                       wkv[..., QK_NOPE:].reshape(kv_lora, -1)], axis=1))
        q_full = _q_up(cgroup, row(mla_q_norm_g), wq_full, cs)
        k_full, v = _kv_up(cgroup, row(mla_kv_norm_g), wkv_full, cs)
        o_b = _mla_attn(q_full, k_full, v, batch=batch, seq=seq)

        h = _merge(o_a, o_b, u, h, bf(w_branch_a[l]), bf(w_branch_b[l]), bf(w_out[l]),
                   row(mix_post_g), gate_col=(hg_k + 2 * hg_v) // d)

        w_kv_mem = bf(jnp.concatenate([xa_w_k[l], xa_w_v[l]], axis=1))
        kv_mem = _norm_mm(mem2, row(xa_mem_g), w_kv_mem, BF16, tm=512, tn=w_kv_mem.shape[1],
                          name="mem_kv")
        h = _xattn(h, row(xa_pre_g), bf(xa_w_q[l]), kv_mem, bf(xa_w_o[l]), row(xa_post_g),
                   batch=batch, seq=seq, n_mem=n_mem)

        h = _ffn(h, row(ffn2_pre_g), w2_gate, w2_up, w2_down, row(ffn2_post_g))
    return h.reshape(batch, seq, d)
```

```python
import functools

import jax
import jax.numpy as jnp
from jax import lax
from jax.experimental import pallas as pl
from jax.experimental.pallas import tpu as pltpu

F32 = jnp.float32
BF16 = jnp.bfloat16

CHUNK = 64
HG_HEADS = 16
MLA_HEADS = 16
QK_NOPE = 128
QK_ROPE = 64
V_HEAD = 128
XA_HEADS = 4
ROPE_THETA = 10000.0
FFN_RESIDUAL_WEIGHT = 0.5
EPS = 1e-6
LOG2_E = 1.4426950408889634

LANES = 128
SUBLANES = 8
BF16_ROWS = 16
MIB = 1024 * 1024
VMEM_LIMIT_BYTES = 56 * MIB

IN_PROJ_TILE = 1024
HGRN_CHUNK = 128
ATTN_Q_BLOCK = 512
ATTN_HEADS_PER_STEP = 4


def _cparams(*sem):
    return pltpu.CompilerParams(dimension_semantics=sem, vmem_limit_bytes=VMEM_LIMIT_BYTES)


def _rms(x, g):
    return x * lax.rsqrt(jnp.mean(x * x, axis=-1, keepdims=True) + EPS) * g


def _sigmoid(x):
    return 1.0 / (1.0 + jnp.exp(-x))


def _dot(a, b):
    return jnp.dot(a, b, preferred_element_type=F32)


def _dot_nt(a, b):
    return lax.dot_general(a, b, (((1,), (1,)), ((), ())), preferred_element_type=F32)


def _dot_tn(a, b):
    return lax.dot_general(a, b, (((0,), (0,)), ((), ())), preferred_element_type=F32)


def _resident(shape):
    return pl.BlockSpec(shape, lambda *_: (0,) * len(shape), pipeline_mode=pl.Buffered(1))


def _ffn_accumulate(x_ref, pg_ref, wg_ref, wu_ref, wd_ref, h_ref, acc_ref):
    @pl.when(pl.program_id(1) == 0)
    def _():
        h_ref[...] = _rms(x_ref[...], pg_ref[...]).astype(BF16)
        acc_ref[...] = jnp.zeros_like(acc_ref)

    h = h_ref[...]
    g = _dot(h, wg_ref[...])
    u = _dot(h, wu_ref[...])
    a = (g * _sigmoid(g) * u).astype(BF16)
    acc_ref[...] += _dot(a, wd_ref[...])


def _ffn_kernel(x_ref, pg_ref, wg_ref, wu_ref, wd_ref, og_ref, o_ref, h_ref, acc_ref):
    _ffn_accumulate(x_ref, pg_ref, wg_ref, wu_ref, wd_ref, h_ref, acc_ref)

    @pl.when(pl.program_id(1) == pl.num_programs(1) - 1)
    def _():
        o_ref[...] = x_ref[...] + FFN_RESIDUAL_WEIGHT * _rms(acc_ref[...], og_ref[...])


def _ffn_first_kernel(x_ref, pg_ref, wg_ref, wu_ref, wd_ref, og_ref, ng_ref, cg_ref, cu_ref, cd_ref,
                      o_ref, hn_ref, cg16_ref, cu16_ref, cd16_ref, h_ref, acc_ref):
    _ffn_accumulate(x_ref, pg_ref, wg_ref, wu_ref, wd_ref, h_ref, acc_ref)
    cg16_ref[...] = cg_ref[...].astype(cg16_ref.dtype)
    cu16_ref[...] = cu_ref[...].astype(cu16_ref.dtype)
    cd16_ref[...] = cd_ref[...].astype(cd16_ref.dtype)

    @pl.when(pl.program_id(1) == pl.num_programs(1) - 1)
    def _():
        y = x_ref[...] + FFN_RESIDUAL_WEIGHT * _rms(acc_ref[...], og_ref[...])
        o_ref[...] = y
        hn_ref[...] = _rms(y, ng_ref[...]).astype(hn_ref.dtype)


def _ffn(x, pre_g, w_gate, w_up, w_down, post_g, next_g=None, cast_weights=None, *, tm=512, tf=512):
    t, d = x.shape
    f = w_gate.shape[1]
    n_i, n_j = t // tm, f // tf
    row = pl.BlockSpec((1, d), lambda i, j: (0, 0))
    tile = pl.BlockSpec((tm, d), lambda i, j: (i, 0))
    in_specs = [
        tile,
        row,
        pl.BlockSpec((d, tf), lambda i, j: (0, j)),
        pl.BlockSpec((d, tf), lambda i, j: (0, j)),
        pl.BlockSpec((tf, d), lambda i, j: (j, 0)),
        row,
    ]
    args = [x, pre_g, w_gate, w_up, w_down, post_g]
    if next_g is None:
        body, out_specs, out_shape = _ffn_kernel, tile, jax.ShapeDtypeStruct((t, d), F32)
    else:
        up_rows, down_rows = d // n_i, f // (n_i * n_j)
        assert d % n_i == 0 and f % (n_i * n_j) == 0
        assert up_rows % BF16_ROWS == 0 and down_rows % BF16_ROWS == 0
        up_slice = pl.BlockSpec((up_rows, tf), lambda i, j: (i, j))
        down_slice = pl.BlockSpec((down_rows, d), lambda i, j: (i * n_j + j, 0))
        body = _ffn_first_kernel
        in_specs = in_specs + [row, up_slice, up_slice, down_slice]
        args = args + [next_g, *cast_weights]
        out_specs = [tile, tile, up_slice, up_slice, down_slice]
        out_shape = [jax.ShapeDtypeStruct((t, d), F32), jax.ShapeDtypeStruct((t, d), BF16),
                     jax.ShapeDtypeStruct((d, f), BF16), jax.ShapeDtypeStruct((d, f), BF16),
                     jax.ShapeDtypeStruct((f, d), BF16)]
    return pl.pallas_call(
        body,
        grid=(t // tm, f // tf),
        in_specs=in_specs,
        out_specs=out_specs,
        out_shape=out_shape,
        scratch_shapes=[pltpu.VMEM((tm, d), BF16), pltpu.VMEM((tm, d), F32)],
        compiler_params=_cparams("parallel", "arbitrary"),
        name="ffn",
    )(*args)


def _by_step(table, j):
    out = table[0]
    for step, value in enumerate(table[1:], start=1):
        out = jnp.where(j == step, value, out)
    return out


def _any_step(steps, j):
    hit = j == steps[0]
    for step in steps[1:]:
        hit = hit | (j == step)
    return hit


def _in_proj_kernel(h_ref, w_ref, wk_ref, u_ref, f_ref, c_ref, g_ref, *, plan):
    j = pl.program_id(1)

    @pl.when(_any_step(plan["u_steps"], j))
    def _():
        u_ref[...] = _dot_nt(h_ref[...], w_ref[...].astype(BF16)).astype(u_ref.dtype)

    @pl.when(_any_step(plan["f_steps"], j))
    def _():
        f_ref[...] = _dot_nt(h_ref[...], w_ref[...].astype(BF16))

    @pl.when(_any_step(plan["g_steps"][:-1], j))
    def _():
        g_ref[...] = _dot_nt(h_ref[...], w_ref[...].astype(BF16)).astype(g_ref.dtype)

    @pl.when(j == plan["g_steps"][-1])
    def _():
        w = w_ref[...]
        row = lax.broadcasted_iota(jnp.int32, w.shape, 0)
        w = jnp.where(row < plan["g_tail"], w, 0.0).astype(BF16)
        g_ref[...] = _dot_nt(h_ref[...], w).astype(g_ref.dtype)

    @pl.when(j == plan["c_step"])
    def _():
        h = h_ref[...]
        tn = w_ref.shape[0]
        c_ref[:, :tn] = _dot_nt(h, w_ref[...].astype(BF16)).astype(c_ref.dtype)
        r = _dot_nt(h, wk_ref[...].astype(BF16))
        lane = lax.broadcasted_iota(jnp.int32, r.shape, 1)
        half = QK_ROPE // 2
        pair = jnp.where(lane < QK_ROPE, r,
                         jnp.where(lane < QK_ROPE + half, -pltpu.roll(r, half, 1),
                                   pltpu.roll(r, QK_ROPE + half, 1)))
        c_ref[:, tn:] = pair.astype(c_ref.dtype)


def _in_proj(h, w_t, *, groups, kpe_col, tm=1024, tn=1024):
    t, d = h.shape
    steps = {"u": [], "f": [], "c": [], "g": []}
    tabs = {"u": [], "f": [], "g": []}
    w_tab = []
    for dest, first, count in groups:
        for k in range(count):
            steps[dest].append(len(w_tab))
            w_tab.append(first + k)
            for name in tabs:
                tabs[name].append(max(len(steps[name]) - 1, 0))
    assert len(steps["c"]) == 1
    g_width = w_t.shape[0] - kpe_col
    g_tail = g_width - (len(steps["g"]) - 1) * tn
    assert 0 < g_tail <= tn
    plan = dict(u_steps=steps["u"], f_steps=steps["f"], g_steps=steps["g"], c_step=steps["c"][0],
                g_tail=g_tail)
    return pl.pallas_call(
        functools.partial(_in_proj_kernel, plan=plan),
        grid=(t // tm, len(w_tab)),
        in_specs=[
            pl.BlockSpec((tm, d), lambda i, j: (i, 0)),
            pl.BlockSpec((tn, d), lambda i, j: (_by_step(w_tab, j), 0)),
            pl.BlockSpec((LANES, d), lambda i, j: (kpe_col // LANES, 0), pipeline_mode=pl.Buffered(1)),
        ],
        out_specs=[
            pl.BlockSpec((tm, tn), lambda i, j: (i, _by_step(tabs["u"], j))),
            pl.BlockSpec((tm, tn), lambda i, j: (i, _by_step(tabs["f"], j))),
            pl.BlockSpec((tm, tn + LANES), lambda i, j: (i, 0)),
            pl.BlockSpec((tm, tn), lambda i, j: (i, _by_step(tabs["g"], j))),
        ],
        out_shape=[
            jax.ShapeDtypeStruct((t, len(steps["u"]) * tn), BF16),
            jax.ShapeDtypeStruct((t, len(steps["f"]) * tn), F32),
            jax.ShapeDtypeStruct((t, tn + LANES), BF16),
            jax.ShapeDtypeStruct((t, g_width), BF16),
        ],
        compiler_params=_cparams("parallel", "arbitrary"),
        name="in_proj",
    )(h, w_t, w_t)


def _norm_mm_kernel(x_ref, g_ref, w_ref, o_ref, h_ref):
    @pl.when(pl.program_id(1) == 0)
    def _():
        h_ref[...] = _rms(x_ref[...].astype(F32), g_ref[...]).astype(BF16)

    o_ref[...] = _dot(h_ref[...], w_ref[...]).astype(o_ref.dtype)


def _norm_mm(x, g, w, out_dtype, *, tm, tn, name):
    t, k = x.shape
    n = w.shape[1]
    return pl.pallas_call(
        _norm_mm_kernel,
        grid=(t // tm, n // tn),
        in_specs=[
            pl.BlockSpec((tm, k), lambda i, j: (i, 0)),
            pl.BlockSpec((1, k), lambda i, j: (0, 0)),
            pl.BlockSpec((k, tn), lambda i, j: (0, j)),
        ],
        out_specs=pl.BlockSpec((tm, tn), lambda i, j: (i, j)),
        out_shape=jax.ShapeDtypeStruct((t, n), out_dtype),
        scratch_shapes=[pltpu.VMEM((tm, k), BF16)],
        compiler_params=_cparams("parallel", "arbitrary"),
        name=name,
    )(x, g, w)


def _scan_level_rows(x, m):
    es, xs = [], []
    for lo in range(0, x.shape[0], 2 * m):
        lower, upper = x[lo:lo + m], x[lo + m:lo + 2 * m]
        boundary = lower[m - 1:m, :]
        es += [boundary - lower, upper]
        xs += [lower, upper + boundary]
    return jnp.concatenate(es, axis=0), jnp.concatenate(xs, axis=0)


def _scan_level_sublanes(x3, sub, m):
    if m == 1:
        boundary = jnp.where((sub & 1) == 1, pltpu.roll(x3, 1, 1), x3)
    elif m == 2:
        boundary = jnp.where(sub < 4, x3[:, 1:2, :], x3[:, 5:6, :])
    else:
        boundary = x3[:, 3:4, :]
    upper = (sub & m) != 0
    return jnp.where(upper, x3, boundary - x3), jnp.where(upper, x3 + boundary, x3)


def _hgrn_kernel(q_ref, f_ref, i_ref, og_ref, lb_ref, ng_ref, o_ref, *, chunk):
    seq, kdim = q_ref.shape
    n_chunks = seq // chunk
    tiles = chunk // SUBLANES
    lb = lb_ref[...]
    one_m_lb = 1.0 - lb
    ng = ng_ref[...]
    sub = lax.broadcasted_iota(jnp.int32, (1, SUBLANES, kdim), 1)
    tt = lax.broadcasted_iota(jnp.int32, (chunk, chunk), 0)
    ss = lax.broadcasted_iota(jnp.int32, (chunk, chunk), 1)
    diff = tt ^ ss
    level = jnp.zeros((chunk, chunk), jnp.int32)
    m = 2
    while m < chunk:
        level = level + (diff >= m).astype(jnp.int32)
        m *= 2
    level = jnp.where(tt > ss, level, jnp.where(tt == ss, -1, -2))

    def body(c, st):
        sl = pl.ds(pl.multiple_of(c * chunk, chunk), chunk)
        fr = f_ref[sl, :]
        e = jnp.exp(-jnp.abs(fr))
        r = 1.0 / (1.0 + e)
        er = e * r
        nonneg = fr >= 0.0
        sig = jnp.where(nonneg, r, er)
        sig_neg = jnp.where(nonneg, er, r)
        x = jnp.log(lb + one_m_lb * sig) * LOG2_E
        kk = one_m_lb * sig_neg
        qf = q_ref[sl, :].astype(F32)
        qq = qf * _sigmoid(qf)
        vv = i_ref[sl, :]
        qq16, kk16 = qq.astype(BF16), kk.astype(BF16)

        def pair_scores(e2):
            w = jnp.exp2(e2).astype(BF16)
            return _dot_nt(qq16 * w, kk16 * w)

        x3 = x.reshape(tiles, SUBLANES, kdim)
        a = jnp.where(level == -1, jnp.sum(qq * kk, axis=-1, keepdims=True), 0.0)
        m, lvl = 1, 0
        while m < SUBLANES:
            e3, x3 = _scan_level_sublanes(x3, sub, m)
            a = jnp.where(level == lvl, pair_scores(e3.reshape(chunk, kdim)), a)
            m, lvl = 2 * m, lvl + 1
        x = x3.reshape(chunk, kdim)
        while m < chunk:
            e2, x = _scan_level_rows(x, m)
            p = pair_scores(e2)
            blocks = []
            for lo in range(0, chunk, 2 * m):
                up = slice(lo + m, lo + 2 * m)
                blocks += [a[lo:lo + m], jnp.where(level[up] == lvl, p[up], a[up])]
            a = jnp.concatenate(blocks, axis=0)
            m, lvl = 2 * m, lvl + 1
        b = x

        o = _dot(a.astype(BF16), vv) + _dot_nt((qq * jnp.exp2(b)).astype(BF16), st.astype(BF16))
        b_last = b[chunk - 1:chunk, :]
        kd = (kk * jnp.exp2(b_last - b)).astype(BF16)
        st = jnp.exp2(b_last) * st + _dot_tn(vv, kd)

        og = og_ref[sl, :].astype(F32)
        o_ref[sl, :] = (_rms(o, ng) * (og * _sigmoid(og))).astype(o_ref.dtype)
        return st

    lax.fori_loop(0, n_chunks, body, jnp.zeros((kdim, kdim), F32), unroll=8)


def _hgrn(u, f_raw, lb, norm_g, *, batch, seq):
    t, width = f_raw.shape
    hd = width // HG_HEADS
    u3 = u.reshape(batch, seq, u.shape[1])
    f3 = f_raw.reshape(batch, seq, width)

    def col(group):
        return pl.BlockSpec((None, seq, hd), lambda b, h, g=group: (b, 0, g * HG_HEADS + h))

    out = pl.pallas_call(
        functools.partial(_hgrn_kernel, chunk=HGRN_CHUNK),
        grid=(batch, HG_HEADS),
        in_specs=[
            col(0),
            pl.BlockSpec((None, seq, hd), lambda b, h: (b, 0, h)),
            col(1),
            col(2),
            pl.BlockSpec((1, hd), lambda b, h: (0, h)),
            pl.BlockSpec((1, hd), lambda b, h: (0, h)),
        ],
        out_specs=pl.BlockSpec((None, seq, hd), lambda b, h: (b, 0, h)),
        out_shape=jax.ShapeDtypeStruct((batch, seq, width), BF16),
        compiler_params=_cparams("parallel", "parallel"),
        name="hgrn2",
    )(u3, f3, u3, u3, lb, norm_g)
    return out.reshape(t, width)


def _rotate(pair, cs):
    y = pair * cs
    return y + pltpu.roll(y, QK_ROPE, 1)


def _q_up_kernel(c_ref, g_ref, w_ref, cs_ref, o_ref):
    h = _rms(c_ref[...].astype(F32), g_ref[...]).astype(BF16)
    cs = cs_ref[...]
    head_w = QK_NOPE + 2 * QK_ROPE
    for hd in range(MLA_HEADS):
        lo = hd * head_w
        acc = _dot(h, w_ref[:, lo:lo + head_w])
        o_ref[:, lo:lo + QK_NOPE] = acc[:, :QK_NOPE].astype(o_ref.dtype)
        o_ref[:, lo + QK_NOPE:lo + head_w] = _rotate(acc[:, QK_NOPE:], cs).astype(o_ref.dtype)


def _q_up(cgroup, g, w, cs, *, tm=512):
    t = cgroup.shape[0]
    k = g.shape[1]
    n = w.shape[1]
    return pl.pallas_call(
        _q_up_kernel,
        grid=(t // tm,),
        in_specs=[
            pl.BlockSpec((tm, k), lambda i: (i, 0)),
            _resident((1, k)),
            _resident((k, n)),
            pl.BlockSpec((tm, LANES), lambda i: (i, 0)),
        ],
        out_specs=pl.BlockSpec((tm, n), lambda i: (i, 0)),
        out_shape=jax.ShapeDtypeStruct((t, n), BF16),
        compiler_params=_cparams("parallel"),
        name="mla_q_up",
    )(cgroup, g, w, cs)


def _kv_up_kernel(ckv_ref, kpe_ref, g_ref, w_ref, cs_ref, k_ref, v_ref):
    h = _rms(ckv_ref[...].astype(F32), g_ref[...]).astype(BF16)
    rot = _rotate(kpe_ref[...].astype(F32), cs_ref[...])
    lane = lax.broadcasted_iota(jnp.int32, rot.shape, 1)
    k_pe = jnp.where(lane < QK_ROPE, rot, 0.0).astype(k_ref.dtype)
    head_w = QK_NOPE + 2 * QK_ROPE
    nope_w = MLA_HEADS * QK_NOPE
    k_nope = _dot(h, w_ref[:, :nope_w])
    v = _dot(h, w_ref[:, nope_w:])
    ones_col = jnp.where(lane == 0, 1.0, 0.0).astype(v_ref.dtype)
    for hd in range(MLA_HEADS):
        lo = hd * head_w
        k_ref[:, lo:lo + QK_NOPE] = k_nope[:, hd * QK_NOPE:(hd + 1) * QK_NOPE].astype(k_ref.dtype)
        k_ref[:, lo + QK_NOPE:lo + head_w] = k_pe
        lo = hd * 2 * V_HEAD
        v_ref[:, lo:lo + V_HEAD] = v[:, hd * V_HEAD:(hd + 1) * V_HEAD].astype(v_ref.dtype)
        v_ref[:, lo + V_HEAD:lo + 2 * V_HEAD] = ones_col


def _kv_up(cgroup, g, w, cs, *, tm=512):
    t = cgroup.shape[0]
    k = g.shape[1]
    n_k = MLA_HEADS * (QK_NOPE + 2 * QK_ROPE)
    n_v = MLA_HEADS * 2 * V_HEAD
    return pl.pallas_call(
        _kv_up_kernel,
        grid=(t // tm,),
        in_specs=[
            pl.BlockSpec((tm, k), lambda i: (i, 1)),
            pl.BlockSpec((tm, LANES), lambda i: (i, 2 * k // LANES)),
            _resident((1, k)),
            _resident((k, w.shape[1])),
            pl.BlockSpec((tm, LANES), lambda i: (i, 0)),
        ],
        out_specs=[pl.BlockSpec((tm, n_k), lambda i: (i, 0)),
                   pl.BlockSpec((tm, n_v), lambda i: (i, 0))],
        out_shape=[jax.ShapeDtypeStruct((t, n_k), BF16), jax.ShapeDtypeStruct((t, n_v), BF16)],
        compiler_params=_cparams("parallel"),
        name="mla_kv_up",
    )(cgroup, cgroup, g, w, cs)


def _mla_attn_kernel(q_ref, k_ref, v_ref, o_ref, *, scale, tq, chunk):
    seq = q_ref.shape[0]
    row_chunk = lax.broadcasted_iota(jnp.int32, (tq, tq), 0) // chunk
    col_chunk = lax.broadcasted_iota(jnp.int32, (tq, tq), 1) // chunk
    visible = col_chunk <= row_chunk
    scale2 = scale * LOG2_E
    head_w = QK_NOPE + 2 * QK_ROPE
    for iq in range(seq // tq):
        lo = iq * tq
        for hd in range(q_ref.shape[1] // head_w):
            qk = slice(hd * head_w, (hd + 1) * head_w)
            vc = slice(hd * 2 * V_HEAD, (hd + 1) * 2 * V_HEAD)
            q = q_ref[lo:lo + tq, qk]
            s_diag = jnp.where(visible, _dot_nt(q, k_ref[lo:lo + tq, qk]) * scale2, -jnp.inf)
            m = jnp.max(s_diag, axis=-1, keepdims=True)
            if iq > 0:
                s_past = _dot_nt(q, k_ref[0:lo, qk]) * scale2
                m = jnp.maximum(m, jnp.max(s_past, axis=-1, keepdims=True))
                acc = _dot(jnp.exp2(s_past - m).astype(BF16), v_ref[0:lo, vc])
                acc = acc + _dot(jnp.exp2(s_diag - m).astype(BF16), v_ref[lo:lo + tq, vc])
            else:
                acc = _dot(jnp.exp2(s_diag - m).astype(BF16), v_ref[lo:lo + tq, vc])
            out = acc[:, :V_HEAD] * (1.0 / acc[:, V_HEAD:V_HEAD + 1])
            o_ref[lo:lo + tq, hd * V_HEAD:(hd + 1) * V_HEAD] = out.astype(o_ref.dtype)


def _mla_attn(q, k, v, *, batch, seq):
    t = q.shape[0]
    head_w = ATTN_HEADS_PER_STEP * (QK_NOPE + 2 * QK_ROPE)
    v_w = ATTN_HEADS_PER_STEP * V_HEAD
    q3 = q.reshape(batch, seq, q.shape[1])
    k3 = k.reshape(batch, seq, k.shape[1])
    v3 = v.reshape(batch, seq, v.shape[1])
    out = pl.pallas_call(
        functools.partial(_mla_attn_kernel, scale=(QK_NOPE + QK_ROPE) ** -0.5,
                          tq=ATTN_Q_BLOCK, chunk=CHUNK),
        grid=(batch, MLA_HEADS // ATTN_HEADS_PER_STEP),
        in_specs=[
            pl.BlockSpec((None, seq, head_w), lambda b, h: (b, 0, h)),
            pl.BlockSpec((None, seq, head_w), lambda b, h: (b, 0, h)),
            pl.BlockSpec((None, seq, 2 * v_w), lambda b, h: (b, 0, h)),
        ],
        out_specs=pl.BlockSpec((None, seq, v_w), lambda b, h: (b, 0, h)),
        out_shape=jax.ShapeDtypeStruct((batch, seq, MLA_HEADS * V_HEAD), BF16),
        compiler_params=_cparams("parallel", "parallel"),
        name="mla_attn",
    )(q3, k3, v3)
    return out.reshape(t, MLA_HEADS * V_HEAD)


def _merge_kernel(oa_ref, ob_ref, g_ref, x_ref, wa_ref, wb_ref, wo_ref, pg_ref, o_ref, *, gate_lo):
    d = x_ref.shape[1]
    g = g_ref[...].astype(F32)
    ya = _dot(oa_ref[...], wa_ref[...])
    yb = _dot(ob_ref[...], wb_ref[...])
    y = (_sigmoid(g[:, gate_lo:gate_lo + d]) * ya
         + _sigmoid(g[:, gate_lo + d:gate_lo + 2 * d]) * yb)
    z = _dot(y.astype(BF16), wo_ref[...])
    o_ref[...] = x_ref[...] + _rms(z, pg_ref[...])


def _merge(o_a, o_b, gates, x, w_a, w_b, w_o, post_g, *, gate_lo, tm=256):
    t, d = x.shape
    return pl.pallas_call(
        functools.partial(_merge_kernel, gate_lo=gate_lo),
        grid=(t // tm,),
        in_specs=[
            pl.BlockSpec((tm, d), lambda i: (i, 0)),
            pl.BlockSpec((tm, d), lambda i: (i, 0)),
            pl.BlockSpec((tm, gates.shape[1]), lambda i: (i, 0)),
            pl.BlockSpec((tm, d), lambda i: (i, 0)),
            _resident(w_a.shape),
            _resident(w_b.shape),
            _resident(w_o.shape),
            _resident((1, d)),
        ],
        out_specs=pl.BlockSpec((tm, d), lambda i: (i, 0)),
        out_shape=jax.ShapeDtypeStruct((t, d), F32),
        compiler_params=_cparams("parallel"),
        name="merge",
    )(o_a, o_b, gates, x, w_a, w_b, w_o, post_g)


def _xattn_kernel(x_ref, pg_ref, wq_ref, kv_ref, wo_ref, og_ref, o_ref, *, scale):
    x = x_ref[...]
    q = _dot(_rms(x, pg_ref[...]).astype(BF16), wq_ref[...]).astype(BF16)
    width = wq_ref.shape[1]
    hd = width // XA_HEADS
    heads = []
    for h in range(XA_HEADS):
        s = _dot_nt(q[:, h * hd:(h + 1) * hd], kv_ref[:, h * hd:(h + 1) * hd]) * scale
        p = jnp.exp(s - jnp.max(s, axis=-1, keepdims=True))
        denom = jnp.sum(p, axis=-1, keepdims=True)
        o = _dot(p.astype(BF16), kv_ref[:, width + h * hd:width + (h + 1) * hd])
        heads.append((o * (1.0 / denom)).astype(BF16))
    z = _dot(jnp.concatenate(heads, axis=-1), wo_ref[...])
    o_ref[...] = x + _rms(z, og_ref[...])


def _xattn(x, pre_g, w_q, kv_mem, w_o, post_g, *, batch, seq, n_mem, tm=512):
    t, d = x.shape
    width = w_q.shape[1]
    x3 = x.reshape(batch, seq, d)
    kv3 = kv_mem.reshape(batch, n_mem, 2 * width)
    out = pl.pallas_call(
        functools.partial(_xattn_kernel, scale=(width // XA_HEADS) ** -0.5),
        grid=(batch, seq // tm),
        in_specs=[
            pl.BlockSpec((None, tm, d), lambda b, i: (b, i, 0)),
            _resident((1, d)),
            _resident(w_q.shape),
            pl.BlockSpec((None, n_mem, 2 * width), lambda b, i: (b, 0, 0)),
            _resident(w_o.shape),
            _resident((1, d)),
        ],
        out_specs=pl.BlockSpec((None, tm, d), lambda b, i: (b, i, 0)),
        out_shape=jax.ShapeDtypeStruct((batch, seq, d), F32),
        compiler_params=_cparams("parallel", "parallel"),
        name="xattn",
    )(x3, pre_g, w_q, kv3, w_o, post_g)
    return out.reshape(t, d)


def _rotate_half_columns(w):
    half = w.shape[-1] // 2
    return jnp.concatenate([-w[..., half:], w[..., :half]], axis=-1)


def kernel(x, mem, positions, hgrn_lb_logits, ffn1_pre_g, ffn1_w_gate, ffn1_w_up, ffn1_w_down, ffn1_post_g, mix_pre_g, w_in, hg_norm_g, mla_q_norm_g, mla_w_q_up, mla_kv_norm_g, mla_w_kv_up, w_branch_a, w_branch_b, w_out, mix_post_g, xa_pre_g, xa_mem_g, xa_w_q, xa_w_k, xa_w_v, xa_w_o, xa_post_g, ffn2_pre_g, ffn2_w_gate, ffn2_w_up, ffn2_w_down, ffn2_post_g):
    batch, seq, d = x.shape
    n_mem = mem.shape[1]
    depth = ffn1_w_gate.shape[0]
    t = batch * seq
    q_lora = mla_q_norm_g.shape[1]
    kv_lora = mla_kv_norm_g.shape[1]
    hg_k = hgrn_lb_logits.shape[1]
    hg_v = hg_norm_g.shape[1]
    assert q_lora == kv_lora, "cgroup column blocks assume equal latent widths"
    assert hg_k == hg_v == d and d // HG_HEADS == LANES

    inv_freq = 1.0 / (ROPE_THETA ** (jnp.arange(0, QK_ROPE, 2, dtype=F32) / QK_ROPE))
    ang = positions.astype(F32).reshape(t, 1) * jnp.tile(inv_freq, 4)
    cs = jnp.where(jnp.arange(2 * QK_ROPE) < QK_ROPE, jnp.cos(ang), jnp.sin(ang))
    lower_bounds = jnp.cumsum(jax.nn.softmax(hgrn_lb_logits.astype(F32), axis=0), axis=0)

    o_q, o_f, o_i, o_og = 0, hg_k, 2 * hg_k, 2 * hg_k + hg_v
    o_cq = o_og + hg_v
    o_ckv = o_cq + q_lora
    o_kpe = o_ckv + kv_lora
    o_ga = o_kpe + QK_ROPE
    o_gb = o_ga + d

    h = x.reshape(t, d)
    mem2 = mem.reshape(batch * n_mem, d)
    for l in range(depth):
        row = lambda g: g[l][None, :]
        bf = lambda w: w.astype(BF16)

        h, h_mix, w2_gate, w2_up, w2_down = _ffn(
            h, row(ffn1_pre_g), bf(ffn1_w_gate[l]), bf(ffn1_w_up[l]), bf(ffn1_w_down[l]),
            row(ffn1_post_g), row(mix_pre_g), (ffn2_w_gate[l], ffn2_w_up[l], ffn2_w_down[l]))

        tn = IN_PROJ_TILE
        assert q_lora + kv_lora == tn and o_cq % tn == 0 and o_kpe == o_cq + tn
        groups = (("u", o_q // tn, hg_k // tn), ("f", o_f // tn, hg_k // tn),
                  ("u", o_i // tn, 2 * hg_v // tn), ("c", o_cq // tn, 1),
                  ("g", o_kpe // tn, pl.cdiv(w_in.shape[2] - o_kpe, tn)))
        u, f_raw, cgroup, gates = _in_proj(h_mix, w_in[l].T, groups=groups, kpe_col=o_kpe, tn=tn)

        o_a = _hgrn(u, f_raw, lower_bounds[l][None, :], row(hg_norm_g), batch=batch, seq=seq)

        wq = mla_w_q_up[l].reshape(q_lora, MLA_HEADS, QK_NOPE + QK_ROPE)
        wq_pe = wq[..., QK_NOPE:]
        wq_full = bf(jnp.concatenate([wq[..., :QK_NOPE], wq_pe, _rotate_half_columns(wq_pe)], axis=-1)
                     .reshape(q_lora, -1))
        wkv = mla_w_kv_up[l].reshape(kv_lora, MLA_HEADS, QK_NOPE + V_HEAD)
        wkv_full = bf(jnp.concatenate([wkv[..., :QK_NOPE].reshape(kv_lora, -1),
                                       wkv[..., QK_NOPE:].reshape(kv_lora, -1)], axis=1))
        q_full = _q_up(cgroup, row(mla_q_norm_g), wq_full, cs)
        k_full, v = _kv_up(cgroup, row(mla_kv_norm_g), wkv_full, cs)
        o_b = _mla_attn(q_full, k_full, v, batch=batch, seq=seq)

        h = _merge(o_a, o_b, gates, h, bf(w_branch_a[l]), bf(w_branch_b[l]), bf(w_out[l]),
                   row(mix_post_g), gate_lo=o_ga - o_kpe)

        w_kv_mem = bf(jnp.concatenate([xa_w_k[l], xa_w_v[l]], axis=1))
        kv_mem = _norm_mm(mem2, row(xa_mem_g), w_kv_mem, BF16, tm=512, tn=w_kv_mem.shape[1],
                          name="mem_kv")
        h = _xattn(h, row(xa_pre_g), bf(xa_w_q[l]), kv_mem, bf(xa_w_o[l]), row(xa_post_g),
                   batch=batch, seq=seq, n_mem=n_mem)

        h = _ffn(h, row(ffn2_pre_g), w2_gate, w2_up, w2_down, row(ffn2_post_g))
    return h.reshape(batch, seq, d)
```

```python
import functools

import jax
import jax.numpy as jnp
from jax import lax
from jax.experimental import pallas as pl
from jax.experimental.pallas import tpu as pltpu

F32 = jnp.float32
BF16 = jnp.bfloat16

CHUNK = 64
HG_HEADS = 16
MLA_HEADS = 16
QK_NOPE = 128
QK_ROPE = 64
V_HEAD = 128
XA_HEADS = 4
ROPE_THETA = 10000.0
FFN_RESIDUAL_WEIGHT = 0.5
EPS = 1e-6
LOG2_E = 1.4426950408889634

LANES = 128
SUBLANES = 8
BF16_ROWS = 16
MIB = 1024 * 1024
VMEM_LIMIT_BYTES = 56 * MIB

IN_PROJ_TILE = 1024
IN_PROJ_SLAB = 256
HGRN_CHUNK = 128
ATTN_Q_BLOCK = 512
ATTN_KEY_TILE = 512
MIXER_HEADS_PER_STEP = 2


def _cparams(*sem):
    return pltpu.CompilerParams(dimension_semantics=sem, vmem_limit_bytes=VMEM_LIMIT_BYTES)


def _rms(x, g):
    return x * lax.rsqrt(jnp.mean(x * x, axis=-1, keepdims=True) + EPS) * g


def _sigmoid(x):
    return 1.0 / (1.0 + jnp.exp(-x))


def _dot(a, b):
    return jnp.dot(a, b, preferred_element_type=F32)


def _dot_nt(a, b):
    return lax.dot_general(a, b, (((1,), (1,)), ((), ())), preferred_element_type=F32)


def _dot_tn(a, b):
    return lax.dot_general(a, b, (((0,), (0,)), ((), ())), preferred_element_type=F32)


def _resident(shape):
    return pl.BlockSpec(shape, lambda *_: (0,) * len(shape), pipeline_mode=pl.Buffered(1))


def _ffn_accumulate(x_ref, pg_ref, wg_ref, wu_ref, wd_ref, h_ref, acc_ref):
    @pl.when(pl.program_id(1) == 0)
    def _():
        h_ref[...] = _rms(x_ref[...], pg_ref[...]).astype(BF16)
        acc_ref[...] = jnp.zeros_like(acc_ref)

    h = h_ref[...]
    g = _dot(h, wg_ref[...])
    u = _dot(h, wu_ref[...])
    a = (g * _sigmoid(g) * u).astype(BF16)
    acc_ref[...] += _dot(a, wd_ref[...])


def _ffn_kernel(x_ref, pg_ref, wg_ref, wu_ref, wd_ref, og_ref, o_ref, h_ref, acc_ref):
    _ffn_accumulate(x_ref, pg_ref, wg_ref, wu_ref, wd_ref, h_ref, acc_ref)

    @pl.when(pl.program_id(1) == pl.num_programs(1) - 1)
    def _():
        o_ref[...] = x_ref[...] + FFN_RESIDUAL_WEIGHT * _rms(acc_ref[...], og_ref[...])


def _ffn_first_kernel(x_ref, pg_ref, wg_ref, wu_ref, wd_ref, og_ref, ng_ref, cg_ref, cu_ref, cd_ref,
                      o_ref, hn_ref, cg16_ref, cu16_ref, cd16_ref, h_ref, acc_ref):
    _ffn_accumulate(x_ref, pg_ref, wg_ref, wu_ref, wd_ref, h_ref, acc_ref)
    cg16_ref[...] = cg_ref[...].astype(cg16_ref.dtype)
    cu16_ref[...] = cu_ref[...].astype(cu16_ref.dtype)
    cd16_ref[...] = cd_ref[...].astype(cd16_ref.dtype)

    @pl.when(pl.program_id(1) == pl.num_programs(1) - 1)
    def _():
        y = x_ref[...] + FFN_RESIDUAL_WEIGHT * _rms(acc_ref[...], og_ref[...])
        o_ref[...] = y
        hn_ref[...] = _rms(y, ng_ref[...]).astype(hn_ref.dtype)


def _ffn(x, pre_g, w_gate, w_up, w_down, post_g, next_g=None, cast_weights=None, *, tm=512, tf=512):
    t, d = x.shape
    f = w_gate.shape[1]
    n_i, n_j = t // tm, f // tf
    row = pl.BlockSpec((1, d), lambda i, j: (0, 0))
    tile = pl.BlockSpec((tm, d), lambda i, j: (i, 0))
    in_specs = [
        tile,
        row,
        pl.BlockSpec((d, tf), lambda i, j: (0, j)),
        pl.BlockSpec((d, tf), lambda i, j: (0, j)),
        pl.BlockSpec((tf, d), lambda i, j: (j, 0)),
        row,
    ]
    args = [x, pre_g, w_gate, w_up, w_down, post_g]
    if next_g is None:
        body, out_specs, out_shape = _ffn_kernel, tile, jax.ShapeDtypeStruct((t, d), F32)
    else:
        up_rows, down_rows = d // n_i, f // (n_i * n_j)
        assert d % n_i == 0 and f % (n_i * n_j) == 0
        assert up_rows % BF16_ROWS == 0 and down_rows % BF16_ROWS == 0
        up_slice = pl.BlockSpec((up_rows, tf), lambda i, j: (i, j))
        down_slice = pl.BlockSpec((down_rows, d), lambda i, j: (i * n_j + j, 0))
        body = _ffn_first_kernel
        in_specs = in_specs + [row, up_slice, up_slice, down_slice]
        args = args + [next_g, *cast_weights]
        out_specs = [tile, tile, up_slice, up_slice, down_slice]
        out_shape = [jax.ShapeDtypeStruct((t, d), F32), jax.ShapeDtypeStruct((t, d), BF16),
                     jax.ShapeDtypeStruct((d, f), BF16), jax.ShapeDtypeStruct((d, f), BF16),
                     jax.ShapeDtypeStruct((f, d), BF16)]
    return pl.pallas_call(
        body,
        grid=(t // tm, f // tf),
        in_specs=in_specs,
        out_specs=out_specs,
        out_shape=out_shape,
        scratch_shapes=[pltpu.VMEM((tm, d), BF16), pltpu.VMEM((tm, d), F32)],
        compiler_params=_cparams("parallel", "arbitrary"),
        name="ffn",
    )(*args)


def _by_step(table, j):
    out = table[0]
    for step, value in enumerate(table[1:], start=1):
        out = jnp.where(j == step, value, out)
    return out


def _any_step(steps, j):
    hit = j == steps[0]
    for step in steps[1:]:
        hit = hit | (j == step)
    return hit


def _project_tile(h_ref, w_ref, o_ref):
    h = h_ref[...]
    for lo in range(0, w_ref.shape[0], IN_PROJ_SLAB):
        w = w_ref[lo:lo + IN_PROJ_SLAB, :].astype(BF16)
        o_ref[:, lo:lo + IN_PROJ_SLAB] = _dot_nt(h, w).astype(o_ref.dtype)


def _in_proj_kernel(h_ref, w_ref, wk_ref, wt_ref, u_ref, f_ref, c_ref, g_ref, *, plan):
    j = pl.program_id(1)

    @pl.when(_any_step(plan["u_steps"], j))
    def _():
        _project_tile(h_ref, w_ref, u_ref)

    @pl.when(_any_step(plan["f_steps"], j))
    def _():
        _project_tile(h_ref, w_ref, f_ref)

    @pl.when(_any_step(plan["g_steps"], j))
    def _():
        _project_tile(h_ref, w_ref, g_ref)

    @pl.when(j == plan["c_step"])
    def _():
        h = h_ref[...]
        tn = w_ref.shape[0]
        wt = wt_ref[...]
        row = lax.broadcasted_iota(jnp.int32, wt.shape, 0)
        wt = jnp.where(row < plan["g_tail"], wt, 0.0).astype(BF16)
        g_ref[:, :wt.shape[0]] = _dot_nt(h, wt).astype(g_ref.dtype)
        _project_tile(h_ref, w_ref, c_ref)
        r = _dot_nt(h, wk_ref[...].astype(BF16))
        lane = lax.broadcasted_iota(jnp.int32, r.shape, 1)
        half = QK_ROPE // 2
        pair = jnp.where(lane < QK_ROPE, r,
                         jnp.where(lane < QK_ROPE + half, -pltpu.roll(r, half, 1),
                                   pltpu.roll(r, QK_ROPE + half, 1)))
        c_ref[:, tn:] = pair.astype(c_ref.dtype)


def _in_proj(h, w_t, *, groups, kpe_col, tm=1024, tn=1024):
    t, d = h.shape
    steps = {"u": [], "f": [], "c": [], "g": []}
    tabs = {"u": [], "f": [], "g": []}
    w_tab = []
    for dest, first, count in groups:
        for k in range(count):
            steps[dest].append(len(w_tab))
            w_tab.append(first + k)
            for name in tabs:
                tabs[name].append(max(len(steps[name]) - 1, 0))
    assert steps["c"] == [len(w_tab) - 1]
    g_width = w_t.shape[0] - kpe_col
    g_tail = g_width - len(steps["g"]) * tn
    assert 0 < g_tail <= LANES and (kpe_col + g_width - g_tail) % LANES == 0
    tabs["g"][-1] = len(steps["g"])
    plan = dict(u_steps=steps["u"], f_steps=steps["f"], g_steps=steps["g"], c_step=steps["c"][0],
                g_tail=g_tail)
    return pl.pallas_call(
        functools.partial(_in_proj_kernel, plan=plan),
        grid=(t // tm, len(w_tab)),
        in_specs=[
            pl.BlockSpec((tm, d), lambda i, j: (i, 0)),
            pl.BlockSpec((tn, d), lambda i, j: (_by_step(w_tab, j), 0)),
            pl.BlockSpec((LANES, d), lambda i, j: (kpe_col // LANES, 0), pipeline_mode=pl.Buffered(1)),
            pl.BlockSpec((LANES, d), lambda i, j: ((kpe_col + g_width - g_tail) // LANES, 0),
                         pipeline_mode=pl.Buffered(1)),
        ],
        out_specs=[
            pl.BlockSpec((tm, tn), lambda i, j: (i, _by_step(tabs["u"], j))),
            pl.BlockSpec((tm, tn), lambda i, j: (i, _by_step(tabs["f"], j))),
            pl.BlockSpec((tm, tn + LANES), lambda i, j: (i, 0)),
            pl.BlockSpec((tm, tn), lambda i, j: (i, _by_step(tabs["g"], j))),
        ],
        out_shape=[
            jax.ShapeDtypeStruct((t, len(steps["u"]) * tn), BF16),
            jax.ShapeDtypeStruct((t, len(steps["f"]) * tn), F32),
            jax.ShapeDtypeStruct((t, tn + LANES), BF16),
            jax.ShapeDtypeStruct((t, g_width), BF16),
        ],
        compiler_params=_cparams("parallel", "arbitrary"),
        name="in_proj",
    )(h, w_t, w_t, w_t)


def _norm_mm_kernel(x_ref, g_ref, w_ref, o_ref, h_ref):
    @pl.when(pl.program_id(1) == 0)
    def _():
        h_ref[...] = _rms(x_ref[...].astype(F32), g_ref[...]).astype(BF16)

    o_ref[...] = _dot(h_ref[...], w_ref[...]).astype(o_ref.dtype)


def _norm_mm(x, g, w, out_dtype, *, tm, tn, name):
    t, k = x.shape
    n = w.shape[1]
    return pl.pallas_call(
        _norm_mm_kernel,
        grid=(t // tm, n // tn),
        in_specs=[
            pl.BlockSpec((tm, k), lambda i, j: (i, 0)),
            pl.BlockSpec((1, k), lambda i, j: (0, 0)),
            pl.BlockSpec((k, tn), lambda i, j: (0, j)),
        ],
        out_specs=pl.BlockSpec((tm, tn), lambda i, j: (i, j)),
        out_shape=jax.ShapeDtypeStruct((t, n), out_dtype),
        scratch_shapes=[pltpu.VMEM((tm, k), BF16)],
        compiler_params=_cparams("parallel", "arbitrary"),
        name=name,
    )(x, g, w)


def _scan_level_rows(x, m):
    es, xs = [], []
    for lo in range(0, x.shape[0], 2 * m):
        lower, upper = x[lo:lo + m], x[lo + m:lo + 2 * m]
        boundary = lower[m - 1:m, :]
        es += [boundary - lower, upper]
        xs += [lower, upper + boundary]
    return jnp.concatenate(es, axis=0), jnp.concatenate(xs, axis=0)


def _scan_level_sublanes(x3, sub, m):
    if m == 1:
        boundary = jnp.where((sub & 1) == 1, pltpu.roll(x3, 1, 1), x3)
    elif m == 2:
        boundary = jnp.where(sub < 4, x3[:, 1:2, :], x3[:, 5:6, :])
    else:
        boundary = x3[:, 3:4, :]
    upper = (sub & m) != 0
    return jnp.where(upper, x3, boundary - x3), jnp.where(upper, x3 + boundary, x3)


def _hgrn_level_table(chunk):
    tt = lax.broadcasted_iota(jnp.int32, (chunk, chunk), 0)
    ss = lax.broadcasted_iota(jnp.int32, (chunk, chunk), 1)
    diff = tt ^ ss
    level = jnp.zeros((chunk, chunk), jnp.int32)
    m = 2
    while m < chunk:
        level = level + (diff >= m).astype(jnp.int32)
        m *= 2
    return jnp.where(tt > ss, level, jnp.where(tt == ss, -1, -2))


def _hgrn_chunk(q_ref, f_ref, i_ref, og_ref, o_ref, rows, cols, lb, ng, sub, level, st):
    chunk, kdim = level.shape[0], lb.shape[1]
    tiles = chunk // SUBLANES
    one_m_lb = 1.0 - lb
    fr = f_ref[rows, cols]
    e = jnp.exp(-jnp.abs(fr))
    r = 1.0 / (1.0 + e)
    er = e * r
    nonneg = fr >= 0.0
    sig = jnp.where(nonneg, r, er)
    sig_neg = jnp.where(nonneg, er, r)
    x = jnp.log(lb + one_m_lb * sig) * LOG2_E
    kk = one_m_lb * sig_neg
    qf = q_ref[rows, cols].astype(F32)
    qq = qf * _sigmoid(qf)
    vv = i_ref[rows, cols]
    qq16, kk16 = qq.astype(BF16), kk.astype(BF16)

    def pair_scores(e2):
        w = jnp.exp2(e2).astype(BF16)
        return _dot_nt(qq16 * w, kk16 * w)

    x3 = x.reshape(tiles, SUBLANES, kdim)
    a = jnp.where(level == -1, jnp.sum(qq * kk, axis=-1, keepdims=True), 0.0)
    m, lvl = 1, 0
    while m < SUBLANES:
        e3, x3 = _scan_level_sublanes(x3, sub, m)
        a = jnp.where(level == lvl, pair_scores(e3.reshape(chunk, kdim)), a)
        m, lvl = 2 * m, lvl + 1
    x = x3.reshape(chunk, kdim)
    while m < chunk:
        e2, x = _scan_level_rows(x, m)
        p = pair_scores(e2)
        blocks = []
        for lo in range(0, chunk, 2 * m):
            up = slice(lo + m, lo + 2 * m)
            blocks += [a[lo:lo + m], jnp.where(level[up] == lvl, p[up], a[up])]
        a = jnp.concatenate(blocks, axis=0)
        m, lvl = 2 * m, lvl + 1
    b = x

    o = _dot(a.astype(BF16), vv) + _dot_nt((qq * jnp.exp2(b)).astype(BF16), st.astype(BF16))
    b_last = b[chunk - 1:chunk, :]
    kd = (kk * jnp.exp2(b_last - b)).astype(BF16)
    st = jnp.exp2(b_last) * st + _dot_tn(vv, kd)

    og = og_ref[rows, cols].astype(F32)
    o_ref[rows, cols] = (_rms(o, ng) * (og * _sigmoid(og))).astype(o_ref.dtype)
    return st


def _rotate(pair, cs):
    y = pair * cs
    return y + pltpu.roll(y, QK_ROPE, 1)


def _q_up_kernel(c_ref, g_ref, w_ref, cs_ref, o_ref, *, scale):
    h = _rms(c_ref[...].astype(F32), g_ref[...]).astype(BF16)
    cs = cs_ref[...]
    head_w = QK_NOPE + 2 * QK_ROPE
    for hd in range(MLA_HEADS):
        lo = hd * head_w
        acc = _dot(h, w_ref[:, lo:lo + head_w]) * scale
        o_ref[:, lo:lo + QK_NOPE] = acc[:, :QK_NOPE].astype(o_ref.dtype)
        o_ref[:, lo + QK_NOPE:lo + head_w] = _rotate(acc[:, QK_NOPE:], cs).astype(o_ref.dtype)


def _q_up(cgroup, g, w, cs, *, tm=512):
    t = cgroup.shape[0]
    k = g.shape[1]
    n = w.shape[1]
    return pl.pallas_call(
        functools.partial(_q_up_kernel, scale=LOG2_E * (QK_NOPE + QK_ROPE) ** -0.5),
        grid=(t // tm,),
        in_specs=[
            pl.BlockSpec((tm, k), lambda i: (i, 0)),
            _resident((1, k)),
            _resident((k, n)),
            pl.BlockSpec((tm, LANES), lambda i: (i, 0)),
        ],
        out_specs=pl.BlockSpec((tm, n), lambda i: (i, 0)),
        out_shape=jax.ShapeDtypeStruct((t, n), BF16),
        compiler_params=_cparams("parallel"),
        name="mla_q_up",
    )(cgroup, g, w, cs)


def _kv_up_kernel(ckv_ref, kpe_ref, g_ref, w_ref, cs_ref, k_ref, v_ref):
    h = _rms(ckv_ref[...].astype(F32), g_ref[...]).astype(BF16)
    rot = _rotate(kpe_ref[...].astype(F32), cs_ref[...])
    lane = lax.broadcasted_iota(jnp.int32, rot.shape, 1)
    k_pe = jnp.where(lane < QK_ROPE, rot, 0.0).astype(k_ref.dtype)
    head_w = QK_NOPE + 2 * QK_ROPE
    nope_w = MLA_HEADS * QK_NOPE
    k_nope = _dot(h, w_ref[:, :nope_w])
    v_ref[...] = _dot(h, w_ref[:, nope_w:]).astype(v_ref.dtype)
    for hd in range(MLA_HEADS):
        lo = hd * head_w
        k_ref[:, lo:lo + QK_NOPE] = k_nope[:, hd * QK_NOPE:(hd + 1) * QK_NOPE].astype(k_ref.dtype)
        k_ref[:, lo + QK_NOPE:lo + head_w] = k_pe


def _kv_up(cgroup, g, w, cs, *, tm=512):
    t = cgroup.shape[0]
    k = g.shape[1]
    n_k = MLA_HEADS * (QK_NOPE + 2 * QK_ROPE)
    n_v = MLA_HEADS * V_HEAD
    return pl.pallas_call(
        _kv_up_kernel,
        grid=(t // tm,),
        in_specs=[
            pl.BlockSpec((tm, k), lambda i: (i, 1)),
            pl.BlockSpec((tm, LANES), lambda i: (i, 2 * k // LANES)),
            _resident((1, k)),
            _resident((k, w.shape[1])),
            pl.BlockSpec((tm, LANES), lambda i: (i, 0)),
        ],
        out_specs=[pl.BlockSpec((tm, n_k), lambda i: (i, 0)),
                   pl.BlockSpec((tm, n_v), lambda i: (i, 0))],
        out_shape=[jax.ShapeDtypeStruct((t, n_k), BF16), jax.ShapeDtypeStruct((t, n_v), BF16)],
        compiler_params=_cparams("parallel"),
        name="mla_kv_up",
    )(cgroup, cgroup, g, w, cs)


def _attn_steps(q_ref, k_ref, v_ref, o_ref, hd, tq, visible):
    head_w = QK_NOPE + 2 * QK_ROPE
    qk = slice(hd * head_w, (hd + 1) * head_w)
    vc = slice(hd * V_HEAD, (hd + 1) * V_HEAD)
    tk = ATTN_KEY_TILE
    lane = lax.broadcasted_iota(jnp.int32, (tk, LANES), 1)
    ones_col = jnp.where(lane == 0, 1.0, 0.0).astype(v_ref.dtype)
    for lo in range(0, q_ref.shape[0], tq):
        q = q_ref[lo:lo + tq, qk]
        scores, m = [], None
        for ko in range(0, lo + tq, tk):
            s = _dot_nt(q, k_ref[ko:ko + tk, qk])
            if ko >= lo:
                s = jnp.where(visible[:, ko - lo:ko - lo + tk], s, -jnp.inf)
            tile_max = jnp.max(s, axis=-1, keepdims=True)
            m = tile_max if m is None else jnp.maximum(m, tile_max)
            scores.append(s)
            yield
        acc = None
        for ko, s in zip(range(0, lo + tq, tk), scores):
            v_ext = jnp.concatenate([v_ref[ko:ko + tk, vc], ones_col], axis=1)
            pv = _dot(jnp.exp2(s - m).astype(BF16), v_ext)
            acc = pv if acc is None else acc + pv
            yield
        out = acc[:, :V_HEAD] * (1.0 / acc[:, V_HEAD:V_HEAD + 1])
        o_ref[lo:lo + tq, hd * V_HEAD:(hd + 1) * V_HEAD] = out.astype(o_ref.dtype)


def _hgrn_steps(q_ref, f_ref, i_ref, og_ref, lb_ref, ng_ref, o_ref, hd, chunk, sub, level):
    kdim = LANES
    cols = slice(hd * kdim, (hd + 1) * kdim)
    st = jnp.zeros((kdim, kdim), F32)
    for lo in range(0, f_ref.shape[0], chunk):
        st = _hgrn_chunk(q_ref, f_ref, i_ref, og_ref, o_ref, slice(lo, lo + chunk), cols,
                         lb_ref[:, cols], ng_ref[:, cols], sub, level, st)
        yield


def _interleave(streams):
    streams = [(list_len, gen) for list_len, gen in streams]
    done = [0] * len(streams)
    total = max(n for n, _ in streams)
    for tick in range(1, total + 1):
        for idx, (n, gen) in enumerate(streams):
            while done[idx] * total < tick * n:
                next(gen)
                done[idx] += 1
    for _, gen in streams:
        for _ in gen:
            raise AssertionError("stream longer than declared")


def _mixers_kernel(qh_ref, f_ref, i_ref, og_ref, lb_ref, ng_ref, q_ref, k_ref, v_ref,
                   oa_ref, ob_ref, *, chunk, tq, causal_chunk):
    seq = f_ref.shape[0]
    heads = f_ref.shape[1] // LANES
    sub = lax.broadcasted_iota(jnp.int32, (1, SUBLANES, LANES), 1)
    level = _hgrn_level_table(chunk)
    row_chunk = lax.broadcasted_iota(jnp.int32, (tq, tq), 0) // causal_chunk
    col_chunk = lax.broadcasted_iota(jnp.int32, (tq, tq), 1) // causal_chunk
    visible = col_chunk <= row_chunk
    n_q = seq // tq
    streams = []
    for hd in range(heads):
        streams.append((seq // chunk, _hgrn_steps(qh_ref, f_ref, i_ref, og_ref, lb_ref, ng_ref, oa_ref,
                                                  hd, chunk, sub, level)))
        n_pieces = n_q * (n_q + 1) * (tq // ATTN_KEY_TILE)
        streams.append((n_pieces, _attn_steps(q_ref, k_ref, v_ref, ob_ref, hd, tq, visible)))
    _interleave(streams)


def _mixers(u, f_raw, lb, norm_g, q, k, v, *, batch, seq):
    t, width = f_raw.shape
    assert HG_HEADS == MLA_HEADS and width // HG_HEADS == LANES
    hp = MIXER_HEADS_PER_STEP
    groups = HG_HEADS // hp
    head_w = hp * (QK_NOPE + 2 * QK_ROPE)
    u3 = u.reshape(batch, seq, u.shape[1])
    f3 = f_raw.reshape(batch, seq, width)
    q3 = q.reshape(batch, seq, q.shape[1])
    k3 = k.reshape(batch, seq, k.shape[1])
    v3 = v.reshape(batch, seq, v.shape[1])

    def col(group):
        return pl.BlockSpec((None, seq, hp * LANES), lambda b, h, g=group: (b, 0, g * groups + h))

    heads = lambda w: pl.BlockSpec((None, seq, w), lambda b, h: (b, 0, h))
    o_a, o_b = pl.pallas_call(
        functools.partial(_mixers_kernel, chunk=HGRN_CHUNK, tq=ATTN_Q_BLOCK, causal_chunk=CHUNK),
        grid=(batch, groups),
        in_specs=[
            col(0),
            heads(hp * LANES),
            col(1),
            col(2),
            pl.BlockSpec((1, hp * LANES), lambda b, h: (0, h)),
            pl.BlockSpec((1, hp * LANES), lambda b, h: (0, h)),
            heads(head_w),
            heads(head_w),
            heads(hp * V_HEAD),
        ],
        out_specs=[heads(hp * LANES), heads(hp * V_HEAD)],
        out_shape=[jax.ShapeDtypeStruct((batch, seq, width), BF16),
                   jax.ShapeDtypeStruct((batch, seq, MLA_HEADS * V_HEAD), BF16)],
        compiler_params=_cparams("parallel", "parallel"),
        name="mixers",
    )(u3, f3, u3, u3, lb, norm_g, q3, k3, v3)
    return o_a.reshape(t, width), o_b.reshape(t, MLA_HEADS * V_HEAD)


def _merge_kernel(oa_ref, ob_ref, g_ref, x_ref, wa_ref, wb_ref, wo_ref, pg_ref, o_ref, *, gate_lo):
    d = x_ref.shape[1]
    g = g_ref[...].astype(F32)
    ya = _dot(oa_ref[...], wa_ref[...])
    yb = _dot(ob_ref[...], wb_ref[...])
    y = (_sigmoid(g[:, gate_lo:gate_lo + d]) * ya
         + _sigmoid(g[:, gate_lo + d:gate_lo + 2 * d]) * yb)
    z = _dot(y.astype(BF16), wo_ref[...])
    o_ref[...] = x_ref[...] + _rms(z, pg_ref[...])


def _merge(o_a, o_b, gates, x, w_a, w_b, w_o, post_g, *, gate_lo, tm=256):
    t, d = x.shape
    return pl.pallas_call(
        functools.partial(_merge_kernel, gate_lo=gate_lo),
        grid=(t // tm,),
        in_specs=[
            pl.BlockSpec((tm, d), lambda i: (i, 0)),
            pl.BlockSpec((tm, d), lambda i: (i, 0)),
            pl.BlockSpec((tm, gates.shape[1]), lambda i: (i, 0)),
            pl.BlockSpec((tm, d), lambda i: (i, 0)),
            _resident(w_a.shape),
            _resident(w_b.shape),
            _resident(w_o.shape),
            _resident((1, d)),
        ],
        out_specs=pl.BlockSpec((tm, d), lambda i: (i, 0)),
        out_shape=jax.ShapeDtypeStruct((t, d), F32),
        compiler_params=_cparams("parallel"),
        name="merge",
    )(o_a, o_b, gates, x, w_a, w_b, w_o, post_g)


def _xattn_kernel(x_ref, pg_ref, wq_ref, kv_ref, wo_ref, og_ref, o_ref, *, scale):
    x = x_ref[...]
    q = _dot(_rms(x, pg_ref[...]).astype(BF16), wq_ref[...]).astype(BF16)
    width = wq_ref.shape[1]
    hd = width // XA_HEADS
    heads = []
    for h in range(XA_HEADS):
        s = _dot_nt(q[:, h * hd:(h + 1) * hd], kv_ref[:, h * hd:(h + 1) * hd]) * scale
        p = jnp.exp(s - jnp.max(s, axis=-1, keepdims=True))
        denom = jnp.sum(p, axis=-1, keepdims=True)
        o = _dot(p.astype(BF16), kv_ref[:, width + h * hd:width + (h + 1) * hd])
        heads.append((o * (1.0 / denom)).astype(BF16))
    z = _dot(jnp.concatenate(heads, axis=-1), wo_ref[...])
    o_ref[...] = x + _rms(z, og_ref[...])


def _xattn(x, pre_g, w_q, kv_mem, w_o, post_g, *, batch, seq, n_mem, tm=512):
    t, d = x.shape
    width = w_q.shape[1]
    x3 = x.reshape(batch, seq, d)
    kv3 = kv_mem.reshape(batch, n_mem, 2 * width)
    out = pl.pallas_call(
        functools.partial(_xattn_kernel, scale=(width // XA_HEADS) ** -0.5),
        grid=(batch, seq // tm),
        in_specs=[
            pl.BlockSpec((None, tm, d), lambda b, i: (b, i, 0)),
            _resident((1, d)),
            _resident(w_q.shape),
            pl.BlockSpec((None, n_mem, 2 * width), lambda b, i: (b, 0, 0)),
            _resident(w_o.shape),
            _resident((1, d)),
        ],
        out_specs=pl.BlockSpec((None, tm, d), lambda b, i: (b, i, 0)),
        out_shape=jax.ShapeDtypeStruct((batch, seq, d), F32),
        compiler_params=_cparams("parallel", "parallel"),
        name="xattn",
    )(x3, pre_g, w_q, kv3, w_o, post_g)
    return out.reshape(t, d)


def _rotate_half_columns(w):
    half = w.shape[-1] // 2
    return jnp.concatenate([-w[..., half:], w[..., :half]], axis=-1)


def kernel(x, mem, positions, hgrn_lb_logits, ffn1_pre_g, ffn1_w_gate, ffn1_w_up, ffn1_w_down, ffn1_post_g, mix_pre_g, w_in, hg_norm_g, mla_q_norm_g, mla_w_q_up, mla_kv_norm_g, mla_w_kv_up, w_branch_a, w_branch_b, w_out, mix_post_g, xa_pre_g, xa_mem_g, xa_w_q, xa_w_k, xa_w_v, xa_w_o, xa_post_g, ffn2_pre_g, ffn2_w_gate, ffn2_w_up, ffn2_w_down, ffn2_post_g):
    batch, seq, d = x.shape
    n_mem = mem.shape[1]
    depth = ffn1_w_gate.shape[0]
    t = batch * seq
    q_lora = mla_q_norm_g.shape[1]
    kv_lora = mla_kv_norm_g.shape[1]
    hg_k = hgrn_lb_logits.shape[1]
    hg_v = hg_norm_g.shape[1]
    assert q_lora == kv_lora, "cgroup column blocks assume equal latent widths"
    assert hg_k == hg_v == d and d // HG_HEADS == LANES

    inv_freq = 1.0 / (ROPE_THETA ** (jnp.arange(0, QK_ROPE, 2, dtype=F32) / QK_ROPE))
    ang = positions.astype(F32).reshape(t, 1) * jnp.tile(inv_freq, 4)
    cs = jnp.where(jnp.arange(2 * QK_ROPE) < QK_ROPE, jnp.cos(ang), jnp.sin(ang))
    lower_bounds = jnp.cumsum(jax.nn.softmax(hgrn_lb_logits.astype(F32), axis=0), axis=0)

    o_q, o_f, o_i, o_og = 0, hg_k, 2 * hg_k, 2 * hg_k + hg_v
    o_cq = o_og + hg_v
    o_ckv = o_cq + q_lora
    o_kpe = o_ckv + kv_lora
    o_ga = o_kpe + QK_ROPE
    o_gb = o_ga + d

    h = x.reshape(t, d)
    mem2 = mem.reshape(batch * n_mem, d)
    for l in range(depth):
        row = lambda g: g[l][None, :]
        bf = lambda w: w.astype(BF16)

        h, h_mix, w2_gate, w2_up, w2_down = _ffn(
            h, row(ffn1_pre_g), bf(ffn1_w_gate[l]), bf(ffn1_w_up[l]), bf(ffn1_w_down[l]),
            row(ffn1_post_g), row(mix_pre_g), (ffn2_w_gate[l], ffn2_w_up[l], ffn2_w_down[l]))

        tn = IN_PROJ_TILE
        assert q_lora + kv_lora == tn and o_cq % tn == 0 and o_kpe == o_cq + tn
        groups = (("u", o_q // tn, hg_k // tn), ("f", o_f // tn, hg_k // tn),
                  ("u", o_i // tn, 2 * hg_v // tn),
                  ("g", o_kpe // tn, (w_in.shape[2] - o_kpe) // tn), ("c", o_cq // tn, 1))
        u, f_raw, cgroup, gates = _in_proj(h_mix, w_in[l].T, groups=groups, kpe_col=o_kpe, tn=tn)


        wq = mla_w_q_up[l].reshape(q_lora, MLA_HEADS, QK_NOPE + QK_ROPE)
        wq_pe = wq[..., QK_NOPE:]
        wq_full = bf(jnp.concatenate([wq[..., :QK_NOPE], wq_pe, _rotate_half_columns(wq_pe)], axis=-1)
                     .reshape(q_lora, -1))
        wkv = mla_w_kv_up[l].reshape(kv_lora, MLA_HEADS, QK_NOPE + V_HEAD)
        wkv_full = bf(jnp.concatenate([wkv[..., :QK_NOPE].reshape(kv_lora, -1),
                                       wkv[..., QK_NOPE:].reshape(kv_lora, -1)], axis=1))
        q_full = _q_up(cgroup, row(mla_q_norm_g), wq_full, cs)
        k_full, v = _kv_up(cgroup, row(mla_kv_norm_g), wkv_full, cs)
        o_a, o_b = _mixers(u, f_raw, lower_bounds[l][None, :], row(hg_norm_g), q_full, k_full, v,
                           batch=batch, seq=seq)

        h = _merge(o_a, o_b, gates, h, bf(w_branch_a[l]), bf(w_branch_b[l]), bf(w_out[l]),
                   row(mix_post_g), gate_lo=o_ga - o_kpe)

        w_kv_mem = bf(jnp.concatenate([xa_w_k[l], xa_w_v[l]], axis=1))
        kv_mem = _norm_mm(mem2, row(xa_mem_g), w_kv_mem, BF16, tm=512, tn=w_kv_mem.shape[1],
                          name="mem_kv")
        h = _xattn(h, row(xa_pre_g), bf(xa_w_q[l]), kv_mem, bf(xa_w_o[l]), row(xa_post_g),
                   batch=batch, seq=seq, n_mem=n_mem)

        h = _ffn(h, row(ffn2_pre_g), w2_gate, w2_up, w2_down, row(ffn2_post_g))
    return h.reshape(batch, seq, d)
```

```python
import functools

import jax
import jax.numpy as jnp
from jax import lax
from jax.experimental import pallas as pl
from jax.experimental.pallas import tpu as pltpu

F32 = jnp.float32
BF16 = jnp.bfloat16

CHUNK = 64
HG_HEADS = 16
MLA_HEADS = 16
QK_NOPE = 128
QK_ROPE = 64
V_HEAD = 128
XA_HEADS = 4
ROPE_THETA = 10000.0
FFN_RESIDUAL_WEIGHT = 0.5
EPS = 1e-6
LOG2_E = 1.4426950408889634

LANES = 128
SUBLANES = 8
BF16_ROWS = 16
MIB = 1024 * 1024
VMEM_LIMIT_BYTES = 56 * MIB

IN_PROJ_TILE = 1024
IN_PROJ_SLAB = 256
HGRN_CHUNK = 128
ATTN_Q_BLOCK = 512
ATTN_KEY_TILE = 512
MIXER_HEADS_PER_STEP = 2


def _cparams(*sem):
    return pltpu.CompilerParams(dimension_semantics=sem, vmem_limit_bytes=VMEM_LIMIT_BYTES)


def _rms(x, g):
    return x * lax.rsqrt(jnp.mean(x * x, axis=-1, keepdims=True) + EPS) * g


def _sigmoid(x):
    return 1.0 / (1.0 + jnp.exp(-x))


def _dot(a, b):
    return jnp.dot(a, b, preferred_element_type=F32)


def _dot_nt(a, b):
    return lax.dot_general(a, b, (((1,), (1,)), ((), ())), preferred_element_type=F32)


def _dot_tn(a, b):
    return lax.dot_general(a, b, (((0,), (0,)), ((), ())), preferred_element_type=F32)


def _resident(shape):
    return pl.BlockSpec(shape, lambda *_: (0,) * len(shape), pipeline_mode=pl.Buffered(1))


def _ffn_accumulate(x_ref, pg_ref, wg_ref, wu_ref, wd_ref, h_ref, acc_ref):
    @pl.when(pl.program_id(1) == 0)
    def _():
        h_ref[...] = _rms(x_ref[...], pg_ref[...]).astype(BF16)
        acc_ref[...] = jnp.zeros_like(acc_ref)

    h = h_ref[...]
    g = _dot(h, wg_ref[...])
    u = _dot(h, wu_ref[...])
    a = (g * _sigmoid(g) * u).astype(BF16)
    acc_ref[...] += _dot(a, wd_ref[...])


def _ffn_kernel(x_ref, h_ref, wg_ref, wu_ref, wd_ref, og_ref, o_ref, acc_ref):
    @pl.when(pl.program_id(1) == 0)
    def _():
        acc_ref[...] = jnp.zeros_like(acc_ref)

    h = h_ref[...]
    g = _dot(h, wg_ref[...])
    u = _dot(h, wu_ref[...])
    a = (g * _sigmoid(g) * u).astype(BF16)
    acc_ref[...] += _dot(a, wd_ref[...])

    @pl.when(pl.program_id(1) == pl.num_programs(1) - 1)
    def _():
        o_ref[...] = x_ref[...] + FFN_RESIDUAL_WEIGHT * _rms(acc_ref[...], og_ref[...])


def _ffn_first_kernel(x_ref, pg_ref, wg_ref, wu_ref, wd_ref, og_ref, ng_ref, cg_ref, cu_ref, cd_ref,
                      o_ref, hn_ref, cg16_ref, cu16_ref, cd16_ref, h_ref, acc_ref):
    _ffn_accumulate(x_ref, pg_ref, wg_ref, wu_ref, wd_ref, h_ref, acc_ref)
    cg16_ref[...] = cg_ref[...].astype(cg16_ref.dtype)
    cu16_ref[...] = cu_ref[...].astype(cu16_ref.dtype)
    cd16_ref[...] = cd_ref[...].astype(cd16_ref.dtype)

    @pl.when(pl.program_id(1) == pl.num_programs(1) - 1)
    def _():
        y = x_ref[...] + FFN_RESIDUAL_WEIGHT * _rms(acc_ref[...], og_ref[...])
        o_ref[...] = y
        hn_ref[...] = _rms(y, ng_ref[...]).astype(hn_ref.dtype)


def _ffn(x, pre_g, w_gate, w_up, w_down, post_g, next_g=None, cast_weights=None, *, tm=512, tf=512):
    t, d = x.shape
    f = w_gate.shape[1]
    n_i, n_j = t // tm, f // tf
    row = pl.BlockSpec((1, d), lambda i, j: (0, 0))
    tile = pl.BlockSpec((tm, d), lambda i, j: (i, 0))
    in_specs = [
        tile,
        row,
        pl.BlockSpec((d, tf), lambda i, j: (0, j)),
        pl.BlockSpec((d, tf), lambda i, j: (0, j)),
        pl.BlockSpec((tf, d), lambda i, j: (j, 0)),
        row,
    ]
    args = [x, pre_g, w_gate, w_up, w_down, post_g]
    scratch = [pltpu.VMEM((tm, d), BF16), pltpu.VMEM((tm, d), F32)]
    if next_g is None:
        body, out_specs, out_shape = _ffn_kernel, tile, jax.ShapeDtypeStruct((t, d), F32)
        in_specs[1] = tile
        scratch = scratch[1:]
    else:
        up_rows, down_rows = d // n_i, f // (n_i * n_j)
        assert d % n_i == 0 and f % (n_i * n_j) == 0
        assert up_rows % BF16_ROWS == 0 and down_rows % BF16_ROWS == 0
        up_slice = pl.BlockSpec((up_rows, tf), lambda i, j: (i, j))
        down_slice = pl.BlockSpec((down_rows, d), lambda i, j: (i * n_j + j, 0))
        body = _ffn_first_kernel
        in_specs = in_specs + [row, up_slice, up_slice, down_slice]
        args = args + [next_g, *cast_weights]
        out_specs = [tile, tile, up_slice, up_slice, down_slice]
        out_shape = [jax.ShapeDtypeStruct((t, d), F32), jax.ShapeDtypeStruct((t, d), BF16),
                     jax.ShapeDtypeStruct((d, f), BF16), jax.ShapeDtypeStruct((d, f), BF16),
                     jax.ShapeDtypeStruct((f, d), BF16)]
    return pl.pallas_call(
        body,
        grid=(t // tm, f // tf),
        in_specs=in_specs,
        out_specs=out_specs,
        out_shape=out_shape,
        scratch_shapes=scratch,
        compiler_params=_cparams("parallel", "arbitrary"),
        name="ffn",
    )(*args)


def _by_step(table, j):
    out = table[0]
    for step, value in enumerate(table[1:], start=1):
        out = jnp.where(j == step, value, out)
    return out


def _any_step(steps, j):
    hit = j == steps[0]
    for step in steps[1:]:
        hit = hit | (j == step)
    return hit


def _project_tile(h_ref, w_ref, o_ref):
    h = h_ref[...]
    for lo in range(0, w_ref.shape[0], IN_PROJ_SLAB):
        w = w_ref[lo:lo + IN_PROJ_SLAB, :].astype(BF16)
        o_ref[:, lo:lo + IN_PROJ_SLAB] = _dot_nt(h, w).astype(o_ref.dtype)


def _in_proj_kernel(h_ref, w_ref, wk_ref, wt_ref, u_ref, f_ref, c_ref, g_ref, *, plan):
    j = pl.program_id(1)

    @pl.when(_any_step(plan["u_steps"], j))
    def _():
        _project_tile(h_ref, w_ref, u_ref)

    @pl.when(_any_step(plan["f_steps"], j))
    def _():
        _project_tile(h_ref, w_ref, f_ref)

    @pl.when(_any_step(plan["g_steps"], j))
    def _():
        _project_tile(h_ref, w_ref, g_ref)

    @pl.when(j == plan["c_step"])
    def _():
        h = h_ref[...]
        tn = w_ref.shape[0]
        wt = wt_ref[...]
        row = lax.broadcasted_iota(jnp.int32, wt.shape, 0)
        wt = jnp.where(row < plan["g_tail"], wt, 0.0).astype(BF16)
        g_ref[:, :wt.shape[0]] = _dot_nt(h, wt).astype(g_ref.dtype)
        _project_tile(h_ref, w_ref, c_ref)
        r = _dot_nt(h, wk_ref[...].astype(BF16))
        lane = lax.broadcasted_iota(jnp.int32, r.shape, 1)
        half = QK_ROPE // 2
        pair = jnp.where(lane < QK_ROPE, r,
                         jnp.where(lane < QK_ROPE + half, -pltpu.roll(r, half, 1),
                                   pltpu.roll(r, QK_ROPE + half, 1)))
        c_ref[:, tn:] = pair.astype(c_ref.dtype)


def _in_proj(h, w_t, *, groups, kpe_col, tm=1024, tn=1024):
    t, d = h.shape
    steps = {"u": [], "f": [], "c": [], "g": []}
    tabs = {"u": [], "f": [], "g": []}
    w_tab = []
    for dest, first, count in groups:
        for k in range(count):
            steps[dest].append(len(w_tab))
            w_tab.append(first + k)
            for name in tabs:
                tabs[name].append(max(len(steps[name]) - 1, 0))
    assert steps["c"] == [len(w_tab) - 1]
    g_width = w_t.shape[0] - kpe_col
    g_tail = g_width - len(steps["g"]) * tn
    assert 0 < g_tail <= LANES and (kpe_col + g_width - g_tail) % LANES == 0
    tabs["g"][-1] = len(steps["g"])
    plan = dict(u_steps=steps["u"], f_steps=steps["f"], g_steps=steps["g"], c_step=steps["c"][0],
                g_tail=g_tail)
    return pl.pallas_call(
        functools.partial(_in_proj_kernel, plan=plan),
        grid=(t // tm, len(w_tab)),
        in_specs=[
            pl.BlockSpec((tm, d), lambda i, j: (i, 0)),
            pl.BlockSpec((tn, d), lambda i, j: (_by_step(w_tab, j), 0)),
            pl.BlockSpec((LANES, d), lambda i, j: (kpe_col // LANES, 0), pipeline_mode=pl.Buffered(1)),
            pl.BlockSpec((LANES, d), lambda i, j: ((kpe_col + g_width - g_tail) // LANES, 0),
                         pipeline_mode=pl.Buffered(1)),
        ],
        out_specs=[
            pl.BlockSpec((tm, tn), lambda i, j: (i, _by_step(tabs["u"], j))),
            pl.BlockSpec((tm, tn), lambda i, j: (i, _by_step(tabs["f"], j))),
            pl.BlockSpec((tm, tn + LANES), lambda i, j: (i, 0)),
            pl.BlockSpec((tm, tn), lambda i, j: (i, _by_step(tabs["g"], j))),
        ],
        out_shape=[
            jax.ShapeDtypeStruct((t, len(steps["u"]) * tn), BF16),
            jax.ShapeDtypeStruct((t, len(steps["f"]) * tn), F32),
            jax.ShapeDtypeStruct((t, tn + LANES), BF16),
            jax.ShapeDtypeStruct((t, g_width), BF16),
        ],
        compiler_params=_cparams("parallel", "arbitrary"),
        name="in_proj",
    )(h, w_t, w_t, w_t)


def _norm_mm_kernel(x_ref, g_ref, w_ref, o_ref, h_ref):
    @pl.when(pl.program_id(1) == 0)
    def _():
        h_ref[...] = _rms(x_ref[...].astype(F32), g_ref[...]).astype(BF16)

    o_ref[...] = _dot(h_ref[...], w_ref[...]).astype(o_ref.dtype)


def _norm_mm(x, g, w, out_dtype, *, tm, tn, name):
    t, k = x.shape
    n = w.shape[1]
    return pl.pallas_call(
        _norm_mm_kernel,
        grid=(t // tm, n // tn),
        in_specs=[
            pl.BlockSpec((tm, k), lambda i, j: (i, 0)),
            pl.BlockSpec((1, k), lambda i, j: (0, 0)),
            pl.BlockSpec((k, tn), lambda i, j: (0, j)),
        ],
        out_specs=pl.BlockSpec((tm, tn), lambda i, j: (i, j)),
        out_shape=jax.ShapeDtypeStruct((t, n), out_dtype),
        scratch_shapes=[pltpu.VMEM((tm, k), BF16)],
        compiler_params=_cparams("parallel", "arbitrary"),
        name=name,
    )(x, g, w)


def _scan_level_rows(x, m):
    es, xs = [], []
    for lo in range(0, x.shape[0], 2 * m):
        lower, upper = x[lo:lo + m], x[lo + m:lo + 2 * m]
        boundary = lower[m - 1:m, :]
        es += [boundary - lower, upper]
        xs += [lower, upper + boundary]
    return jnp.concatenate(es, axis=0), jnp.concatenate(xs, axis=0)


def _scan_level_sublanes(x3, sub, m):
    if m == 1:
        boundary = jnp.where((sub & 1) == 1, pltpu.roll(x3, 1, 1), x3)
    elif m == 2:
        boundary = jnp.where(sub < 4, x3[:, 1:2, :], x3[:, 5:6, :])
    else:
        boundary = x3[:, 3:4, :]
    upper = (sub & m) != 0
    return jnp.where(upper, x3, boundary - x3), jnp.where(upper, x3 + boundary, x3)


def _hgrn_level_table(chunk):
    tt = lax.broadcasted_iota(jnp.int32, (chunk, chunk), 0)
    ss = lax.broadcasted_iota(jnp.int32, (chunk, chunk), 1)
    diff = tt ^ ss
    level = jnp.zeros((chunk, chunk), jnp.int32)
    m = 2
    while m < chunk:
        level = level + (diff >= m).astype(jnp.int32)
        m *= 2
    return jnp.where(tt > ss, level, jnp.where(tt == ss, -1, -2))


def _hgrn_chunk(q_ref, f_ref, i_ref, og_ref, o_ref, rows, cols, lb, ng, sub, level, st):
    chunk, kdim = level.shape[0], lb.shape[1]
    tiles = chunk // SUBLANES
    one_m_lb = 1.0 - lb
    fr = f_ref[rows, cols]
    e = jnp.exp(-jnp.abs(fr))
    r = 1.0 / (1.0 + e)
    er = e * r
    nonneg = fr >= 0.0
    sig = jnp.where(nonneg, r, er)
    sig_neg = jnp.where(nonneg, er, r)
    x = jnp.log(lb + one_m_lb * sig) * LOG2_E
    kk = one_m_lb * sig_neg
    qf = q_ref[rows, cols].astype(F32)
    qq = qf * _sigmoid(qf)
    vv = i_ref[rows, cols]
    qq16, kk16 = qq.astype(BF16), kk.astype(BF16)

    def pair_scores(e2):
        w = jnp.exp2(e2).astype(BF16)
        return _dot_nt(qq16 * w, kk16 * w)

    x3 = x.reshape(tiles, SUBLANES, kdim)
    a = jnp.where(level == -1, jnp.sum(qq * kk, axis=-1, keepdims=True), 0.0)
    m, lvl = 1, 0
    while m < SUBLANES:
        e3, x3 = _scan_level_sublanes(x3, sub, m)
        a = jnp.where(level == lvl, pair_scores(e3.reshape(chunk, kdim)), a)
        m, lvl = 2 * m, lvl + 1
    x = x3.reshape(chunk, kdim)
    while m < chunk:
        e2, x = _scan_level_rows(x, m)
        p = pair_scores(e2)
        blocks = []
        for lo in range(0, chunk, 2 * m):
            up = slice(lo + m, lo + 2 * m)
            blocks += [a[lo:lo + m], jnp.where(level[up] == lvl, p[up], a[up])]
        a = jnp.concatenate(blocks, axis=0)
        m, lvl = 2 * m, lvl + 1
    b = x

    o = _dot(a.astype(BF16), vv) + _dot_nt((qq * jnp.exp2(b)).astype(BF16), st.astype(BF16))
    b_last = b[chunk - 1:chunk, :]
    kd = (kk * jnp.exp2(b_last - b)).astype(BF16)
    st = jnp.exp2(b_last) * st + _dot_tn(vv, kd)

    og = og_ref[rows, cols].astype(F32)
    o_ref[rows, cols] = (_rms(o, ng) * (og * _sigmoid(og))).astype(o_ref.dtype)
    return st


def _rotate(pair, cs):
    y = pair * cs
    return y + pltpu.roll(y, QK_ROPE, 1)


def _q_up_kernel(c_ref, g_ref, w_ref, cs_ref, o_ref, *, scale):
    h = _rms(c_ref[...].astype(F32), g_ref[...]).astype(BF16)
    cs = cs_ref[...]
    head_w = QK_NOPE + 2 * QK_ROPE
    for hd in range(MLA_HEADS):
        lo = hd * head_w
        acc = _dot(h, w_ref[:, lo:lo + head_w]) * scale
        o_ref[:, lo:lo + QK_NOPE] = acc[:, :QK_NOPE].astype(o_ref.dtype)
        o_ref[:, lo + QK_NOPE:lo + head_w] = _rotate(acc[:, QK_NOPE:], cs).astype(o_ref.dtype)


def _q_up(cgroup, g, w, cs, *, tm=512):
    t = cgroup.shape[0]
    k = g.shape[1]
    n = w.shape[1]
    return pl.pallas_call(
        functools.partial(_q_up_kernel, scale=LOG2_E * (QK_NOPE + QK_ROPE) ** -0.5),
        grid=(t // tm,),
        in_specs=[
            pl.BlockSpec((tm, k), lambda i: (i, 0)),
            _resident((1, k)),
            _resident((k, n)),
            pl.BlockSpec((tm, LANES), lambda i: (i, 0)),
        ],
        out_specs=pl.BlockSpec((tm, n), lambda i: (i, 0)),
        out_shape=jax.ShapeDtypeStruct((t, n), BF16),
        compiler_params=_cparams("parallel"),
        name="mla_q_up",
    )(cgroup, g, w, cs)


def _kv_up_kernel(ckv_ref, kpe_ref, g_ref, w_ref, cs_ref, k_ref, v_ref):
    h = _rms(ckv_ref[...].astype(F32), g_ref[...]).astype(BF16)
    rot = _rotate(kpe_ref[...].astype(F32), cs_ref[...])
    lane = lax.broadcasted_iota(jnp.int32, rot.shape, 1)
    k_pe = jnp.where(lane < QK_ROPE, rot, 0.0).astype(k_ref.dtype)
    head_w = QK_NOPE + 2 * QK_ROPE
    nope_w = MLA_HEADS * QK_NOPE
    k_nope = _dot(h, w_ref[:, :nope_w])
    v_ref[...] = _dot(h, w_ref[:, nope_w:]).astype(v_ref.dtype)
    for hd in range(MLA_HEADS):
        lo = hd * head_w
        k_ref[:, lo:lo + QK_NOPE] = k_nope[:, hd * QK_NOPE:(hd + 1) * QK_NOPE].astype(k_ref.dtype)
        k_ref[:, lo + QK_NOPE:lo + head_w] = k_pe


def _kv_up(cgroup, g, w, cs, *, tm=512):
    t = cgroup.shape[0]
    k = g.shape[1]
    n_k = MLA_HEADS * (QK_NOPE + 2 * QK_ROPE)
    n_v = MLA_HEADS * V_HEAD
    return pl.pallas_call(
        _kv_up_kernel,
        grid=(t // tm,),
        in_specs=[
            pl.BlockSpec((tm, k), lambda i: (i, 1)),
            pl.BlockSpec((tm, LANES), lambda i: (i, 2 * k // LANES)),
            _resident((1, k)),
            _resident((k, w.shape[1])),
            pl.BlockSpec((tm, LANES), lambda i: (i, 0)),
        ],
        out_specs=[pl.BlockSpec((tm, n_k), lambda i: (i, 0)),
                   pl.BlockSpec((tm, n_v), lambda i: (i, 0))],
        out_shape=[jax.ShapeDtypeStruct((t, n_k), BF16), jax.ShapeDtypeStruct((t, n_v), BF16)],
        compiler_params=_cparams("parallel"),
        name="mla_kv_up",
    )(cgroup, cgroup, g, w, cs)


def _attn_steps(q_ref, k_ref, v_ref, o_ref, hd, tq, visible):
    head_w = QK_NOPE + 2 * QK_ROPE
    qk = slice(hd * head_w, (hd + 1) * head_w)
    vc = slice(hd * V_HEAD, (hd + 1) * V_HEAD)
    tk = ATTN_KEY_TILE
    lane = lax.broadcasted_iota(jnp.int32, (tk, LANES), 1)
    ones_col = jnp.where(lane == 0, 1.0, 0.0).astype(v_ref.dtype)
    for lo in range(0, q_ref.shape[0], tq):
        q = q_ref[lo:lo + tq, qk]
        scores, m = [], None
        for ko in range(0, lo + tq, tk):
            s = _dot_nt(q, k_ref[ko:ko + tk, qk])
            if ko >= lo:
                s = jnp.where(visible[:, ko - lo:ko - lo + tk], s, -jnp.inf)
            tile_max = jnp.max(s, axis=-1, keepdims=True)
            m = tile_max if m is None else jnp.maximum(m, tile_max)
            scores.append(s)
            yield
        acc = None
        for ko, s in zip(range(0, lo + tq, tk), scores):
            v_ext = jnp.concatenate([v_ref[ko:ko + tk, vc], ones_col], axis=1)
            pv = _dot(jnp.exp2(s - m).astype(BF16), v_ext)
            acc = pv if acc is None else acc + pv
            yield
        out = acc[:, :V_HEAD] * (1.0 / acc[:, V_HEAD:V_HEAD + 1])
        o_ref[lo:lo + tq, hd * V_HEAD:(hd + 1) * V_HEAD] = out.astype(o_ref.dtype)


def _hgrn_steps(q_ref, f_ref, i_ref, og_ref, lb_ref, ng_ref, o_ref, hd, chunk, sub, level):
    kdim = LANES
    cols = slice(hd * kdim, (hd + 1) * kdim)
    st = jnp.zeros((kdim, kdim), F32)
    for lo in range(0, f_ref.shape[0], chunk):
        st = _hgrn_chunk(q_ref, f_ref, i_ref, og_ref, o_ref, slice(lo, lo + chunk), cols,
                         lb_ref[:, cols], ng_ref[:, cols], sub, level, st)
        yield


def _interleave(streams):
    streams = [(list_len, gen) for list_len, gen in streams]
    done = [0] * len(streams)
    total = max(n for n, _ in streams)
    for tick in range(1, total + 1):
        for idx, (n, gen) in enumerate(streams):
            while done[idx] * total < tick * n:
                next(gen)
                done[idx] += 1
    for _, gen in streams:
        for _ in gen:
            raise AssertionError("stream longer than declared")


def _mixers_kernel(qh_ref, f_ref, i_ref, og_ref, lb_ref, ng_ref, q_ref, k_ref, v_ref,
                   oa_ref, ob_ref, *, chunk, tq, causal_chunk):
    seq = f_ref.shape[0]
    heads = f_ref.shape[1] // LANES
    sub = lax.broadcasted_iota(jnp.int32, (1, SUBLANES, LANES), 1)
    level = _hgrn_level_table(chunk)
    row_chunk = lax.broadcasted_iota(jnp.int32, (tq, tq), 0) // causal_chunk
    col_chunk = lax.broadcasted_iota(jnp.int32, (tq, tq), 1) // causal_chunk
    visible = col_chunk <= row_chunk
    n_q = seq // tq
    streams = []
    for hd in range(heads):
        streams.append((seq // chunk, _hgrn_steps(qh_ref, f_ref, i_ref, og_ref, lb_ref, ng_ref, oa_ref,
                                                  hd, chunk, sub, level)))
        n_pieces = n_q * (n_q + 1) * (tq // ATTN_KEY_TILE)
        streams.append((n_pieces, _attn_steps(q_ref, k_ref, v_ref, ob_ref, hd, tq, visible)))
    _interleave(streams)


def _mixers(u, f_raw, lb, norm_g, q, k, v, *, batch, seq):
    t, width = f_raw.shape
    assert HG_HEADS == MLA_HEADS and width // HG_HEADS == LANES
    hp = MIXER_HEADS_PER_STEP
    groups = HG_HEADS // hp
    head_w = hp * (QK_NOPE + 2 * QK_ROPE)
    u3 = u.reshape(batch, seq, u.shape[1])
    f3 = f_raw.reshape(batch, seq, width)
    q3 = q.reshape(batch, seq, q.shape[1])
    k3 = k.reshape(batch, seq, k.shape[1])
    v3 = v.reshape(batch, seq, v.shape[1])

    def col(group):
        return pl.BlockSpec((None, seq, hp * LANES), lambda b, h, g=group: (b, 0, g * groups + h))

    heads = lambda w: pl.BlockSpec((None, seq, w), lambda b, h: (b, 0, h))
    o_a, o_b = pl.pallas_call(
        functools.partial(_mixers_kernel, chunk=HGRN_CHUNK, tq=ATTN_Q_BLOCK, causal_chunk=CHUNK),
        grid=(batch, groups),
        in_specs=[
            col(0),
            heads(hp * LANES),
            col(1),
            col(2),
            pl.BlockSpec((1, hp * LANES), lambda b, h: (0, h)),
            pl.BlockSpec((1, hp * LANES), lambda b, h: (0, h)),
            heads(head_w),
            heads(head_w),
            heads(hp * V_HEAD),
        ],
        out_specs=[heads(hp * LANES), heads(hp * V_HEAD)],
        out_shape=[jax.ShapeDtypeStruct((batch, seq, width), BF16),
                   jax.ShapeDtypeStruct((batch, seq, MLA_HEADS * V_HEAD), BF16)],
        compiler_params=_cparams("parallel", "parallel"),
        name="mixers",
    )(u3, f3, u3, u3, lb, norm_g, q3, k3, v3)
    return o_a.reshape(t, width), o_b.reshape(t, MLA_HEADS * V_HEAD)


def _merge_kernel(oa_ref, ob_ref, g_ref, x_ref, wa_ref, wb_ref, wo_ref, pg_ref, o_ref, *, gate_lo):
    d = x_ref.shape[1]
    g = g_ref[...].astype(F32)
    ya = _dot(oa_ref[...], wa_ref[...])
    yb = _dot(ob_ref[...], wb_ref[...])
    y = (_sigmoid(g[:, gate_lo:gate_lo + d]) * ya
         + _sigmoid(g[:, gate_lo + d:gate_lo + 2 * d]) * yb)
    z = _dot(y.astype(BF16), wo_ref[...])
    o_ref[...] = x_ref[...] + _rms(z, pg_ref[...])


def _merge(o_a, o_b, gates, x, w_a, w_b, w_o, post_g, *, gate_lo, tm=256):
    t, d = x.shape
    return pl.pallas_call(
        functools.partial(_merge_kernel, gate_lo=gate_lo),
        grid=(t // tm,),
        in_specs=[
            pl.BlockSpec((tm, d), lambda i: (i, 0)),
            pl.BlockSpec((tm, d), lambda i: (i, 0)),
            pl.BlockSpec((tm, gates.shape[1]), lambda i: (i, 0)),
            pl.BlockSpec((tm, d), lambda i: (i, 0)),
            _resident(w_a.shape),
            _resident(w_b.shape),
            _resident(w_o.shape),
            _resident((1, d)),
        ],
        out_specs=pl.BlockSpec((tm, d), lambda i: (i, 0)),
        out_shape=jax.ShapeDtypeStruct((t, d), F32),
        compiler_params=_cparams("parallel"),
        name="merge",
    )(o_a, o_b, gates, x, w_a, w_b, w_o, post_g)


def _xattn_kernel(x_ref, pg_ref, wq_ref, kv_ref, wo_ref, og_ref, ng_ref, o_ref, hn_ref, *, scale):
    x = x_ref[...]
    q = _dot(_rms(x, pg_ref[...]).astype(BF16), wq_ref[...]).astype(BF16)
    width = wq_ref.shape[1]
    hd = width // XA_HEADS
    heads = []
    for h in range(XA_HEADS):
        s = _dot_nt(q[:, h * hd:(h + 1) * hd], kv_ref[:, h * hd:(h + 1) * hd]) * scale
        p = jnp.exp(s - jnp.max(s, axis=-1, keepdims=True))
        denom = jnp.sum(p, axis=-1, keepdims=True)
        o = _dot(p.astype(BF16), kv_ref[:, width + h * hd:width + (h + 1) * hd])
        heads.append((o * (1.0 / denom)).astype(BF16))
    z = _dot(jnp.concatenate(heads, axis=-1), wo_ref[...])
    y = x + _rms(z, og_ref[...])
    o_ref[...] = y
    hn_ref[...] = _rms(y, ng_ref[...]).astype(hn_ref.dtype)


def _xattn(x, pre_g, w_q, kv_mem, w_o, post_g, next_g, *, batch, seq, n_mem, tm=512):
    t, d = x.shape
    width = w_q.shape[1]
    x3 = x.reshape(batch, seq, d)
    kv3 = kv_mem.reshape(batch, n_mem, 2 * width)
    tile = pl.BlockSpec((None, tm, d), lambda b, i: (b, i, 0))
    out, hn = pl.pallas_call(
        functools.partial(_xattn_kernel, scale=(width // XA_HEADS) ** -0.5),
        grid=(batch, seq // tm),
        in_specs=[
            tile,
            _resident((1, d)),
            _resident(w_q.shape),
            pl.BlockSpec((None, n_mem, 2 * width), lambda b, i: (b, 0, 0)),
            _resident(w_o.shape),
            _resident((1, d)),
            _resident((1, d)),
        ],
        out_specs=[tile, tile],
        out_shape=[jax.ShapeDtypeStruct((batch, seq, d), F32),
                   jax.ShapeDtypeStruct((batch, seq, d), BF16)],
        compiler_params=_cparams("parallel", "parallel"),
        name="xattn",
    )(x3, pre_g, w_q, kv3, w_o, post_g, next_g)
    return out.reshape(t, d), hn.reshape(t, d)


def _rotate_half_columns(w):
    half = w.shape[-1] // 2
    return jnp.concatenate([-w[..., half:], w[..., :half]], axis=-1)


def kernel(x, mem, positions, hgrn_lb_logits, ffn1_pre_g, ffn1_w_gate, ffn1_w_up, ffn1_w_down, ffn1_post_g, mix_pre_g, w_in, hg_norm_g, mla_q_norm_g, mla_w_q_up, mla_kv_norm_g, mla_w_kv_up, w_branch_a, w_branch_b, w_out, mix_post_g, xa_pre_g, xa_mem_g, xa_w_q, xa_w_k, xa_w_v, xa_w_o, xa_post_g, ffn2_pre_g, ffn2_w_gate, ffn2_w_up, ffn2_w_down, ffn2_post_g):
    batch, seq, d = x.shape
    n_mem = mem.shape[1]
    depth = ffn1_w_gate.shape[0]
    t = batch * seq
    q_lora = mla_q_norm_g.shape[1]
    kv_lora = mla_kv_norm_g.shape[1]
    hg_k = hgrn_lb_logits.shape[1]
    hg_v = hg_norm_g.shape[1]
    assert q_lora == kv_lora, "cgroup column blocks assume equal latent widths"
    assert hg_k == hg_v == d and d // HG_HEADS == LANES

    inv_freq = 1.0 / (ROPE_THETA ** (jnp.arange(0, QK_ROPE, 2, dtype=F32) / QK_ROPE))
    ang = positions.astype(F32).reshape(t, 1) * jnp.tile(inv_freq, 4)
    cs = jnp.where(jnp.arange(2 * QK_ROPE) < QK_ROPE, jnp.cos(ang), jnp.sin(ang))
    lower_bounds = jnp.cumsum(jax.nn.softmax(hgrn_lb_logits.astype(F32), axis=0), axis=0)

    o_q, o_f, o_i, o_og = 0, hg_k, 2 * hg_k, 2 * hg_k + hg_v
    o_cq = o_og + hg_v
    o_ckv = o_cq + q_lora
    o_kpe = o_ckv + kv_lora
    o_ga = o_kpe + QK_ROPE
    o_gb = o_ga + d

    h = x.reshape(t, d)
    mem2 = mem.reshape(batch * n_mem, d)
    for l in range(depth):
        row = lambda g: g[l][None, :]
        bf = lambda w: w.astype(BF16)

        h, h_mix, w2_gate, w2_up, w2_down = _ffn(
            h, row(ffn1_pre_g), bf(ffn1_w_gate[l]), bf(ffn1_w_up[l]), bf(ffn1_w_down[l]),
            row(ffn1_post_g), row(mix_pre_g), (ffn2_w_gate[l], ffn2_w_up[l], ffn2_w_down[l]))

        tn = IN_PROJ_TILE
        assert q_lora + kv_lora == tn and o_cq % tn == 0 and o_kpe == o_cq + tn
        groups = (("u", o_q // tn, hg_k // tn), ("f", o_f // tn, hg_k // tn),
                  ("u", o_i // tn, 2 * hg_v // tn),
                  ("g", o_kpe // tn, (w_in.shape[2] - o_kpe) // tn), ("c", o_cq // tn, 1))
        u, f_raw, cgroup, gates = _in_proj(h_mix, w_in[l].T, groups=groups, kpe_col=o_kpe, tn=tn)


        wq = mla_w_q_up[l].reshape(q_lora, MLA_HEADS, QK_NOPE + QK_ROPE)
        wq_pe = wq[..., QK_NOPE:]
        wq_full = bf(jnp.concatenate([wq[..., :QK_NOPE], wq_pe, _rotate_half_columns(wq_pe)], axis=-1)
                     .reshape(q_lora, -1))
        wkv = mla_w_kv_up[l].reshape(kv_lora, MLA_HEADS, QK_NOPE + V_HEAD)
        wkv_full = bf(jnp.concatenate([wkv[..., :QK_NOPE].reshape(kv_lora, -1),
                                       wkv[..., QK_NOPE:].reshape(kv_lora, -1)], axis=1))
        q_full = _q_up(cgroup, row(mla_q_norm_g), wq_full, cs)
        k_full, v = _kv_up(cgroup, row(mla_kv_norm_g), wkv_full, cs)
        o_a, o_b = _mixers(u, f_raw, lower_bounds[l][None, :], row(hg_norm_g), q_full, k_full, v,
                           batch=batch, seq=seq)

        h = _merge(o_a, o_b, gates, h, bf(w_branch_a[l]), bf(w_branch_b[l]), bf(w_out[l]),
                   row(mix_post_g), gate_lo=o_ga - o_kpe)

        w_kv_mem = bf(jnp.concatenate([xa_w_k[l], xa_w_v[l]], axis=1))
        kv_mem = _norm_mm(mem2, row(xa_mem_g), w_kv_mem, BF16, tm=512, tn=w_kv_mem.shape[1],
                          name="mem_kv")
        h, h_ffn2 = _xattn(h, row(xa_pre_g), bf(xa_w_q[l]), kv_mem, bf(xa_w_o[l]), row(xa_post_g),
                           row(ffn2_pre_g), batch=batch, seq=seq, n_mem=n_mem)

        h = _ffn(h, h_ffn2, w2_gate, w2_up, w2_down, row(ffn2_post_g))
    return h.reshape(batch, seq, d)
```
